```python
import math
import jax, jax.numpy as jnp
from jax import lax
import numpy as np

D_MODEL = 1024
BATCH = 8
SEQ = 8192
DEPTH = 1
DEC_BATCH = 128
DEC_SEQ = 1
PAST_LEN = 8192
PAGE_SIZE = 128

EPS = 1e-6
D_FF = ((8 * D_MODEL // 3 + 127) // 128) * 128
D_PLE = 256
D_INNER = 2 * D_MODEL
SSM_HEAD_DIM = 64
SSM_HEADS = D_INNER // SSM_HEAD_DIM
SSM_GROUPS = 4
SSM_HPG = SSM_HEADS // SSM_GROUPS
SSM_STATE = 128
CONV_W = 4
CONV_DIM = D_INNER + 2 * SSM_GROUPS * SSM_STATE
SSD_CHUNK = 128
DIL_GROUPS = ((128, 1), (512, 4), (2048, 16))
N_DIL = len(DIL_GROUPS)
HPG = 8
ATTN_HEAD_DIM = 64
ATTN_HEADS = N_DIL * HPG
ATTN_WIDTH = ATTN_HEADS * ATTN_HEAD_DIM
ATTN_OUT = HPG * ATTN_HEAD_DIM
ATTN_SCALE = ATTN_HEAD_DIM ** -0.5
ROPE_THETA = 10000.0
IN_SPLITS = tuple(int(c) for c in np.cumsum([D_INNER, CONV_DIM, SSM_HEADS, ATTN_WIDTH, ATTN_WIDTH, ATTN_WIDTH, D_MODEL]))
N_IN = IN_SPLITS[-1] + D_MODEL

kernel_name = 'hybrid_ssd_dilated_attn_decode_step'


def rms_norm(x, gain):
    xf = x.astype(jnp.float32)
    y = xf * lax.rsqrt(jnp.mean(xf * xf, axis=-1, keepdims=True) + EPS)
    return (y * gain.astype(jnp.float32)).astype(x.dtype)


def swiglu(u, w_gate, w_up, w_down):
    return (jax.nn.silu(u @ w_gate) * (u @ w_up)) @ w_down


def rotary(x, pos):
    half = x.shape[-1] // 2
    inv_freq = ROPE_THETA ** (-jnp.arange(half, dtype=jnp.float32) / half)
    ang = pos.astype(jnp.float32)[:, None] * inv_freq[None, :]
    cos = jnp.cos(ang)[None, :, None, :]
    sin = jnp.sin(ang)[None, :, None, :]
    xf = x.astype(jnp.float32)
    x1, x2 = xf[..., :half], xf[..., half:]
    return jnp.concatenate([x1 * cos - x2 * sin, x2 * cos + x1 * sin], axis=-1).astype(x.dtype)


def causal_conv(xbc, prev, w, b):
    s = xbc.shape[1]
    xp = jnp.concatenate([prev.astype(xbc.dtype), xbc], axis=1)
    out = b
    for tap in range(CONV_W):
        out = out + xp[:, tap:tap + s] * w[tap]
    return jax.nn.silu(out), xp[:, xp.shape[1] - (CONV_W - 1):]


def ssd_chunked(x, dt, a, bm, cm):
    b, s, g, hg, p = x.shape
    n = bm.shape[-1]
    L = SSD_CHUNK
    nc = s // L

    def chunks(t):
        return jnp.moveaxis(t.reshape((b, nc, L) + t.shape[2:]), 1, 0)

    causal = jnp.tril(jnp.ones((L, L), dtype=bool))[None, :, :, None, None]

    def step(state, inp):
        xc, dtc, bc, cc = inp
        acs = jnp.cumsum(dtc * a, axis=1)
        seg = acs[:, :, None] - acs[:, None, :]
        decay = jnp.exp(jnp.where(causal, seg, -jnp.inf))
        cb = jnp.einsum('blgn,bsgn->blsg', cc, bc)
        wts = cb[..., None] * decay * dtc[:, None]
        y = jnp.einsum('blsgh,bsghp->blghp', wts, xc)
        y = y + jnp.einsum('blgn,bghpn->blghp', cc, state) * jnp.exp(acs)[..., None]
        to_end = jnp.exp(acs[:, -1:] - acs) * dtc
        state = jnp.exp(acs[:, -1])[..., None, None] * state + jnp.einsum('bsgh,bsgn,bsghp->bghpn', to_end, bc, xc)
        return state, y

    init = jnp.zeros((b, g, hg, p, n), jnp.float32)
    final, ys = lax.scan(step, init, (chunks(x), chunks(dt), chunks(bm), chunks(cm)))
    return jnp.moveaxis(ys, 0, 1).reshape(b, s, g, hg, p), final


def ssd_recurrent(x, dt, a, bm, cm, state):
    def step(st, inp):
        xt, dtt, bt, ct = inp
        st = jnp.exp(dtt * a)[..., None, None] * st + jnp.einsum('bgh,bgn,bghp->bghpn', dtt, bt, xt)
        return st, jnp.einsum('bgn,bghpn->bghp', ct, st)

    final, ys = lax.scan(step, state, (jnp.moveaxis(x, 1, 0), jnp.moveaxis(dt, 1, 0), jnp.moveaxis(bm, 1, 0), jnp.moveaxis(cm, 1, 0)))
    return jnp.moveaxis(ys, 0, 1), final


def ssd_branch(z, xbc, dt_raw, conv_prev, ssm_prev, conv_w, conv_b, dt_bias, a_log, d_skip, norm_g):
    b, s, _ = z.shape
    xbc, conv_new = causal_conv(xbc, conv_prev, conv_w, conv_b)
    xs, bm, cm = jnp.split(xbc, [D_INNER, D_INNER + SSM_GROUPS * SSM_STATE], axis=-1)
    xs = xs.reshape(b, s, SSM_GROUPS, SSM_HPG, SSM_HEAD_DIM).astype(jnp.float32)
    bm = bm.reshape(b, s, SSM_GROUPS, SSM_STATE).astype(jnp.float32)
    cm = cm.reshape(b, s, SSM_GROUPS, SSM_STATE).astype(jnp.float32)
    dt = jax.nn.softplus(dt_raw.astype(jnp.float32) + dt_bias.astype(jnp.float32)).reshape(b, s, SSM_GROUPS, SSM_HPG)
    a = -jnp.exp(a_log.astype(jnp.float32)).reshape(SSM_GROUPS, SSM_HPG)
    if ssm_prev is None:
        y, st = ssd_chunked(xs, dt, a, bm, cm)
    else:
        st0 = ssm_prev.astype(jnp.float32).reshape(b, SSM_GROUPS, SSM_HPG, SSM_HEAD_DIM, SSM_STATE)
        y, st = ssd_recurrent(xs, dt, a, bm, cm, st0)
    y = y + d_skip.astype(jnp.float32).reshape(SSM_GROUPS, SSM_HPG)[:, :, None] * xs
    y = y.reshape(b, s, D_INNER).astype(z.dtype)
    y = rms_norm(y * jax.nn.silu(z), norm_g)
    return y, conv_new, st.reshape(b, SSM_HEADS, SSM_HEAD_DIM, SSM_STATE)


def dilated_attention_prompt(q, k, v, window, dil):
    b, s, h, dh = q.shape
    nk = window // dil
    sp = -(-s // window) * window
    nb = sp // window

    def to_blocks(t):
        t = jnp.pad(t.astype(jnp.float32), ((0, 0), (0, sp - s), (0, 0), (0, 0)))
        t = t.reshape(b, sp // dil, dil, h, dh).transpose(0, 2, 1, 3, 4)
        return t.reshape(b, dil, nb, nk, h, dh)

    qb, kb, vb = to_blocks(q), to_blocks(k), to_blocks(v)

    def with_prev(t):
        prev = jnp.pad(t, ((0, 0), (0, 0), (1, 0), (0, 0), (0, 0), (0, 0)))[:, :, :-1]
        return jnp.concatenate([prev, t], axis=3)

    kk, vv = with_prev(kb), with_prev(vb)
    scores = jnp.einsum('brnihd,brnjhd->brnhij', qb, kk) * ATTN_SCALE
    qi = jnp.arange(nk)[:, None]
    kj = jnp.arange(2 * nk)[None, :]
    dist = qi + nk - kj
    band = (dist >= 0) & (dist <= nk)
    has_prev = (jnp.arange(nb) > 0)[:, None, None] | (kj >= nk)[None]
    mask = band[None] & has_prev
    scores = jnp.where(mask[None, None, :, None], scores, -jnp.inf)
    m = jnp.max(scores, axis=-1, keepdims=True)
    p = jnp.exp(scores - m)
    l = jnp.sum(p, axis=-1, keepdims=True)
    o = jnp.einsum('brnhij,brnjhd->brnihd', p / l, vv)
    lse = (m + jnp.log(l))[..., 0]
    o = o.reshape(b, dil, sp // dil, h, dh).transpose(0, 2, 1, 3, 4).reshape(b, sp, h, dh)[:, :s]
    lse = lse.transpose(0, 1, 2, 4, 3).reshape(b, dil, sp // dil, h).transpose(0, 2, 1, 3).reshape(b, sp, h)[:, :s]
    return o, lse


def dilated_attention_sample(q, k, v, cache_kv, window, dil):
    lc = cache_kv.shape[1]
    t = q.shape[1]
    nk = window // dil
    keys = jnp.concatenate([cache_kv[:, :, 0].astype(jnp.float32), k.astype(jnp.float32)], axis=1)
    vals = jnp.concatenate([cache_kv[:, :, 1].astype(jnp.float32), v.astype(jnp.float32)], axis=1)
    idx = lc + jnp.arange(t)[:, None] - dil * jnp.arange(nk + 1)[None, :]
    valid = idx >= 0
    idx = jnp.maximum(idx, 0)
    kg = keys[:, idx]
    vg = vals[:, idx]
    scores = jnp.einsum('bthd,btkhd->bthk', q.astype(jnp.float32), kg) * ATTN_SCALE
    scores = jnp.where(valid[None, :, None, :], scores, -jnp.inf)
    m = jnp.max(scores, axis=-1, keepdims=True)
    p = jnp.exp(scores - m)
    l = jnp.sum(p, axis=-1, keepdims=True)
    o = jnp.einsum('bthk,btkhd->bthd', p / l, vg)
    lse = (m + jnp.log(l))[..., 0]
    return o, lse


def decoder_layer(x, p_emb, pos, conv_prev, ssm_prev, kv_prev, prm):
    b, s, _ = x.shape
    h = x + 0.5 * swiglu(rms_norm(x, prm['norm_ffn1']), prm['w_ffn1_gate'], prm['w_ffn1_up'], prm['w_ffn1_down'])
    u = rms_norm(h, prm['norm_mix'])
    z, xbc, dt_raw, q, k, v, gate_ssm, gate_attn = jnp.split(u @ prm['w_in'], IN_SPLITS, axis=-1)
    if conv_prev is None:
        conv_prev = jnp.zeros((b, CONV_W - 1, CONV_DIM), x.dtype)
    y_ssm, conv_new, ssm_new = ssd_branch(z, xbc, dt_raw, conv_prev, ssm_prev, prm['conv_w'], prm['conv_b'],
                                          prm['dt_bias'], prm['a_log'], prm['d_skip'], prm['norm_ssm'])
    q = rotary(q.reshape(b, s, ATTN_HEADS, ATTN_HEAD_DIM), pos)
    k = rotary(k.reshape(b, s, ATTN_HEADS, ATTN_HEAD_DIM), pos)
    v = v.reshape(b, s, ATTN_HEADS, ATTN_HEAD_DIM)
    outs, lses, kv_new = [], [], []
    for gi, (window, dil) in enumerate(DIL_GROUPS):
        sl = slice(gi * HPG, (gi + 1) * HPG)
        qg, kg, vg = q[:, :, sl], k[:, :, sl], v[:, :, sl]
        if kv_prev is None:
            o, lse = dilated_attention_prompt(qg, kg, vg, window, dil)
            keep = min(window, s)
            kv_new.append(jnp.stack([kg[:, s - keep:], vg[:, s - keep:]], axis=2))
        else:
            o, lse = dilated_attention_sample(qg, kg, vg, kv_prev[gi], window, dil)
            kv_new.append(jnp.stack([kg, vg], axis=2))
        outs.append(o)
        lses.append(lse)
    alpha = jax.nn.softmax(jnp.stack(lses), axis=0)
    attn = jnp.einsum('gbsh,gbshd->bshd', alpha, jnp.stack(outs)).reshape(b, s, ATTN_OUT).astype(x.dtype)
    merged = jax.nn.sigmoid(gate_ssm) * (y_ssm @ prm['w_o_ssm']) + jax.nn.sigmoid(gate_attn) * (attn @ prm['w_o_attn'])
    h = h + merged @ prm['w_out']
    h = h + 0.5 * swiglu(rms_norm(h, prm['norm_ffn2']), prm['w_ffn2_gate'], prm['w_ffn2_up'], prm['w_ffn2_down'])
    h = h + jax.nn.sigmoid(rms_norm(h, prm['norm_ple']) @ prm['w_ple_gate']) * (p_emb @ prm['w_ple_proj'])
    return h, kv_new, conv_new, ssm_new


def setup_inputs(seed: int = 0) -> dict:
    key = jax.random.key(seed)
    ks = iter(jax.random.split(key, 48))
    f32 = jnp.float32
    L = DEPTH

    def nrm(shape, scale):
        return jax.random.normal(next(ks), shape, f32) * scale

    def gain(shape):
        return 1.0 + nrm(shape, 0.01)

    dt0 = jnp.exp(jax.random.uniform(next(ks), (L, SSM_HEADS), f32, math.log(1e-3), math.log(1e-1)))
    dt_bias = dt0 + jnp.log(-jnp.expm1(-dt0))
    a_log = jnp.log(jax.random.uniform(next(ks), (L, SSM_HEADS), f32, 1.0, 16.0))
    kv_len = [min(w, PAST_LEN) for (w, _) in DIL_GROUPS]
    return {
        'x_prompt': nrm((BATCH, SEQ, D_MODEL), 1.0),
        'x_sample': nrm((DEC_BATCH, DEC_SEQ, D_MODEL), 1.0),
        'cache_kv_w128': nrm((L, DEC_BATCH, kv_len[0], 2, HPG, ATTN_HEAD_DIM), 1.0),
        'cache_kv_w512': nrm((L, DEC_BATCH, kv_len[1], 2, HPG, ATTN_HEAD_DIM), 1.0),
        'cache_kv_w2048': nrm((L, DEC_BATCH, kv_len[2], 2, HPG, ATTN_HEAD_DIM), 1.0),
        'state_conv': nrm((L, DEC_BATCH, CONV_W - 1, CONV_DIM), 1.0),
        'state_ssm': nrm((L, DEC_BATCH, SSM_HEADS, SSM_HEAD_DIM, SSM_STATE), 0.1),
        'p_prompt': nrm((L, BATCH, SEQ, D_PLE), 1.0),
        'p_sample': nrm((L, DEC_BATCH, DEC_SEQ, D_PLE), 1.0),
        'norm_ffn1': gain((L, D_MODEL)),
        'w_ffn1_gate': nrm((L, D_MODEL, D_FF), D_MODEL ** -0.5),
        'w_ffn1_up': nrm((L, D_MODEL, D_FF), D_MODEL ** -0.5),
        'w_ffn1_down': nrm((L, D_FF, D_MODEL), D_FF ** -0.5),
        'norm_mix': gain((L, D_MODEL)),
        'w_in': nrm((L, D_MODEL, N_IN), D_MODEL ** -0.5),
        'conv_w': nrm((L, CONV_W, CONV_DIM), CONV_W ** -0.5),
        'conv_b': nrm((L, CONV_DIM), 0.01),
        'dt_bias': dt_bias,
        'a_log': a_log,
        'd_skip': gain((L, SSM_HEADS)),
        'norm_ssm': gain((L, D_INNER)),
        'w_o_ssm': nrm((L, D_INNER, D_MODEL), D_INNER ** -0.5),
        'w_o_attn': nrm((L, ATTN_OUT, D_MODEL), ATTN_OUT ** -0.5),
        'w_out': nrm((L, D_MODEL, D_MODEL), D_MODEL ** -0.5),
        'norm_ffn2': gain((L, D_MODEL)),
        'w_ffn2_gate': nrm((L, D_MODEL, D_FF), D_MODEL ** -0.5),
        'w_ffn2_up': nrm((L, D_MODEL, D_FF), D_MODEL ** -0.5),
        'w_ffn2_down': nrm((L, D_FF, D_MODEL), D_FF ** -0.5),
        'norm_ple': gain((L, D_MODEL)),
        'w_ple_gate': nrm((L, D_MODEL, D_MODEL), D_MODEL ** -0.5),
        'w_ple_proj': nrm((L, D_PLE, D_MODEL), D_PLE ** -0.5),
        'norm_final': gain((D_MODEL,)),
    }


def reference(x_prompt, x_sample, cache_kv_w128, cache_kv_w512, cache_kv_w2048, state_conv, state_ssm,
              p_prompt, p_sample, norm_ffn1, w_ffn1_gate, w_ffn1_up, w_ffn1_down, norm_mix, w_in,
              conv_w, conv_b, dt_bias, a_log, d_skip, norm_ssm, w_o_ssm, w_o_attn, w_out,
              norm_ffn2, w_ffn2_gate, w_ffn2_up, w_ffn2_down, norm_ple, w_ple_gate, w_ple_proj, norm_final):
    pos_prompt = jnp.arange(x_prompt.shape[1], dtype=jnp.int32)
    pos_sample = PAST_LEN + jnp.arange(x_sample.shape[1], dtype=jnp.int32)
    hp, hs = x_prompt, x_sample
    kvp_l, convp_l, ssmp_l, kvs_l, convs_l, ssms_l = [], [], [], [], [], []
    for i in range(DEPTH):
        prm = {
            'norm_ffn1': norm_ffn1[i], 'w_ffn1_gate': w_ffn1_gate[i], 'w_ffn1_up': w_ffn1_up[i],
            'w_ffn1_down': w_ffn1_down[i], 'norm_mix': norm_mix[i], 'w_in': w_in[i],
            'conv_w': conv_w[i], 'conv_b': conv_b[i], 'dt_bias': dt_bias[i], 'a_log': a_log[i],
            'd_skip': d_skip[i], 'norm_ssm': norm_ssm[i], 'w_o_ssm': w_o_ssm[i], 'w_o_attn': w_o_attn[i],
            'w_out': w_out[i], 'norm_ffn2': norm_ffn2[i], 'w_ffn2_gate': w_ffn2_gate[i],
            'w_ffn2_up': w_ffn2_up[i], 'w_ffn2_down': w_ffn2_down[i], 'norm_ple': norm_ple[i],
            'w_ple_gate': w_ple_gate[i], 'w_ple_proj': w_ple_proj[i],
        }
        hp, kvp, convp, ssmp = decoder_layer(hp, p_prompt[i], pos_prompt, None, None, None, prm)
        hs, kvs, convs, ssms = decoder_layer(hs, p_sample[i], pos_sample, state_conv[i], state_ssm[i],
                                             (cache_kv_w128[i], cache_kv_w512[i], cache_kv_w2048[i]), prm)
        kvp_l.append(kvp); convp_l.append(convp); ssmp_l.append(ssmp)
        kvs_l.append(kvs); convs_l.append(convs); ssms_l.append(ssms)
    y_prompt = rms_norm(hp, norm_final)
    y_sample = rms_norm(hs, norm_final)
    kv128_p = jnp.stack([kv[0] for kv in kvp_l])
    kv512_p = jnp.stack([kv[1] for kv in kvp_l])
    kv2048_p = jnp.stack([kv[2] for kv in kvp_l])
    kv128_s = jnp.stack([kv[0] for kv in kvs_l])
    kv512_s = jnp.stack([kv[1] for kv in kvs_l])
    kv2048_s = jnp.stack([kv[2] for kv in kvs_l])
    conv_p = jnp.stack(convp_l)
    ssm_p = jnp.stack(ssmp_l)
    conv_s = jnp.stack(convs_l)
    ssm_s = jnp.stack(ssms_l)
    return (y_prompt, y_sample, kv128_p, kv512_p, kv2048_p, conv_p, ssm_p, kv128_s, kv512_s, kv2048_s, conv_s, ssm_s)
```

```python
import functools
import math

import jax
import jax.numpy as jnp
from jax import lax
from jax.experimental import pallas as pl
from jax.experimental.pallas import tpu as pltpu

F32 = jnp.float32
BF16 = jnp.bfloat16

EPS = 1e-6
ROPE_THETA = 10000.0
PAST_LEN = 8192
HEAD_DIM = 64
HPG = 8
SSM_GROUPS = 4
SSM_STATE = 128
CONV_W = 4
DIL_GROUPS = ((128, 1), (512, 4), (2048, 16))
ATTN_SCALE = HEAD_DIM ** -0.5
NK = 128

LANES = 128
SUBLANES = 8
VMEM_LIMIT_CAP = 56 * 1024 * 1024


def _cparams(sem, vmem_bytes):
    return pltpu.CompilerParams(dimension_semantics=sem,
                                vmem_limit_bytes=int(min(VMEM_LIMIT_CAP, vmem_bytes)))


def _resident(shape):
    nd = len(shape)
    return pl.BlockSpec(shape, lambda *_: (0,) * nd, pipeline_mode=pl.Buffered(1))


def _nbytes(shape, dtype):
    return math.prod(shape) * jnp.dtype(dtype).itemsize


def _rms(x, gain):
    ms = jnp.mean(x * x, axis=-1, keepdims=True)
    return x * lax.rsqrt(ms + EPS) * gain


def _sigmoid(x):
    return 1.0 / (1.0 + jnp.exp(-x))


def _silu(x):
    return x * _sigmoid(x)


def _softplus(x):
    return jnp.maximum(x, 0.0) + jnp.log1p(jnp.exp(-jnp.abs(x)))


def _dot(a, b):
    return jnp.dot(a, b, preferred_element_type=F32)


def _dot_nt(a, b):
    return lax.dot_general(a, b, (((1,), (1,)), ((), ())), preferred_element_type=F32)


def _split3(x):
    hi = x.astype(BF16)
    r1 = x - hi.astype(F32)
    mid = r1.astype(BF16)
    lo = (r1 - mid.astype(F32)).astype(BF16)
    return hi, mid, lo


def _dot_sel_l(sel, x):
    hi, mid, lo = _split3(x)
    return _dot(sel, hi) + _dot(sel, mid) + _dot(sel, lo)


def _dot_sel_r(x, sel):
    hi, mid, lo = _split3(x)
    return _dot(hi, sel) + _dot(mid, sel) + _dot(lo, sel)


def _head_expand(n_heads_padded, n_heads):
    r = lax.broadcasted_iota(jnp.int32, (n_heads_padded, n_heads * HEAD_DIM), 0)
    c = lax.broadcasted_iota(jnp.int32, (n_heads_padded, n_heads * HEAD_DIM), 1)
    return (c // HEAD_DIM == r).astype(BF16)


def _head_collapse(n_heads, n_heads_padded):
    r = lax.broadcasted_iota(jnp.int32, (n_heads * HEAD_DIM, n_heads_padded), 0)
    c = lax.broadcasted_iota(jnp.int32, (n_heads * HEAD_DIM, n_heads_padded), 1)
    return (r // HEAD_DIM == c).astype(BF16)


FF_CHUNK = 256


def _swiglu_half(x, gain, wg_ref, wu_ref, wd_ref):
    u = _rms(x, gain).astype(BF16)
    d_ff = wg_ref.shape[1]
    acc = jnp.zeros(x.shape, F32)
    for c in range(d_ff // FF_CHUNK):
        sl = slice(c * FF_CHUNK, (c + 1) * FF_CHUNK)
        g = _dot(u, wg_ref[:, sl])
        up = _dot(u, wu_ref[:, sl])
        acc = acc + _dot((_silu(g) * up).astype(BF16), wd_ref[sl, :])
    return x + 0.5 * acc


def _ffn_kernel(x_ref, n1_ref, wg_ref, wu_ref, wd_ref, n2_ref, h_ref, u_ref):
    h = _swiglu_half(x_ref[...], n1_ref[...], wg_ref, wu_ref, wd_ref)
    h_ref[...] = h
    u_ref[...] = _rms(h, n2_ref[...]).astype(u_ref.dtype)


def _ffn(x, n1, wg, wu, wd, n2, tm):
    t, d = x.shape
    d_ff = wg.shape[1]
    vmem = (3 * _nbytes((d, d_ff), BF16) + 2 * 2 * _nbytes((tm, d), F32) + 2 * _nbytes((tm, d), BF16)
            + 6 * _nbytes((tm, d), F32) + 4 * _nbytes((tm, FF_CHUNK), F32))
    return pl.pallas_call(
        _ffn_kernel,
        grid=(t // tm,),
        in_specs=[pl.BlockSpec((tm, d), lambda i: (i, 0)),
                  _resident((1, d)), _resident((d, d_ff)), _resident((d, d_ff)), _resident((d_ff, d)),
                  _resident((1, d))],
        out_specs=[pl.BlockSpec((tm, d), lambda i: (i, 0)), pl.BlockSpec((tm, d), lambda i: (i, 0))],
        out_shape=[jax.ShapeDtypeStruct((t, d), F32), jax.ShapeDtypeStruct((t, d), BF16)],
        compiler_params=_cparams(("parallel",), vmem),
        name="ffn_half_step",
    )(x, n1, wg, wu, wd, n2)


def _rope_tables(tm, seq, pos0):
    row = lax.broadcasted_iota(jnp.int32, (tm, LANES), 0) + pl.program_id(0) * tm
    pos = (row % seq + pos0).astype(F32)
    lane = lax.broadcasted_iota(jnp.int32, (tm, LANES), 1)
    half = HEAD_DIM // 2
    j = (lane % half).astype(F32)
    inv_freq = jnp.exp(j * (-math.log(ROPE_THETA) / half))
    ang = pos * inv_freq
    first = (lane % HEAD_DIM) < half
    return jnp.cos(ang), jnp.where(first, -jnp.sin(ang), jnp.sin(ang)), first


def _rotary(x, cos, sin_signed, first):
    outs = []
    for c in range(x.shape[1] // LANES):
        xb = x[:, c * LANES:(c + 1) * LANES]
        other = jnp.where(first, pltpu.roll(xb, LANES - HEAD_DIM // 2, 1), pltpu.roll(xb, HEAD_DIM // 2, 1))
        outs.append(xb * cos + other * sin_signed)
    return jnp.concatenate(outs, axis=1)


def _proj_ssm_kernel(u_ref, wz_ref, wx_ref, wdt_ref, z_ref, xbc_ref, dt_ref):
    u = u_ref[...]
    z_ref[...] = _dot(u, wz_ref[...]).astype(z_ref.dtype)
    xbc_ref[...] = _dot(u, wx_ref[...]).astype(xbc_ref.dtype)
    dt_ref[...] = _dot(u, wdt_ref[...])


def _proj_attn_kernel(u_ref, wq_ref, wk_ref, wv_ref, wgs_ref, wga_ref,
                      q_ref, k_ref, v_ref, gs_ref, ga_ref, *, seq, pos0):
    u = u_ref[...]
    cos, sin_signed, first = _rope_tables(u.shape[0], seq, pos0)
    q_ref[...] = _rotary(_dot(u, wq_ref[...]), cos, sin_signed, first).astype(q_ref.dtype)
    k_ref[...] = _rotary(_dot(u, wk_ref[...]), cos, sin_signed, first).astype(k_ref.dtype)
    v_ref[...] = _dot(u, wv_ref[...]).astype(v_ref.dtype)
    gs_ref[...] = _sigmoid(_dot(u, wgs_ref[...])).astype(gs_ref.dtype)
    ga_ref[...] = _sigmoid(_dot(u, wga_ref[...])).astype(ga_ref.dtype)


def _proj(kernel, u, weights, out_dtypes, tm, name):
    t, d = u.shape
    row = lambda i: (i, 0)
    widths = [w.shape[1] for w in weights]
    vmem = (sum(_nbytes(w.shape, BF16) for w in weights) + 2 * _nbytes((tm, d), BF16)
            + sum(2 * _nbytes((tm, n), dt) for n, dt in zip(widths, out_dtypes))
            + 4 * _nbytes((tm, max(widths)), F32))
    return pl.pallas_call(
        kernel,
        grid=(t // tm,),
        in_specs=[pl.BlockSpec((tm, d), row)] + [_resident(w.shape) for w in weights],
        out_specs=[pl.BlockSpec((tm, n), row) for n in widths],
        out_shape=[jax.ShapeDtypeStruct((t, n), dt) for n, dt in zip(widths, out_dtypes)],
        compiler_params=_cparams(("parallel",), vmem),
        name=name,
    )(u, *weights)


SSD_CHUNK = 128
CONV_PAD = SUBLANES


def _ssd_gate_norm(y, z, gain):
    return _rms(y * _silu(z), gain)


def _ssd_prompt_kernel(z_ref, xbc_ref, dt_ref, cw_ref, cb_ref, dtb_ref, alog_ref, dskip_ref, ng_ref,
                       y_ref, st_ref, xpad, st_t):
    L = SSD_CHUNK
    c = pl.program_id(1)
    d_inner = z_ref.shape[-1]
    gw = HPG * HEAD_DIM
    gn = SSM_GROUPS * SSM_STATE

    @pl.when(c == 0)
    def _():
        xpad[0:CONV_PAD, :] = jnp.zeros((CONV_PAD, xpad.shape[1]), F32)
        st_t[...] = jnp.zeros(st_t.shape, F32)

    xpad[CONV_PAD:CONV_PAD + L, :] = xbc_ref[0].astype(F32)
    conv = cb_ref[...] + xpad[CONV_PAD - 3:CONV_PAD - 3 + L, :] * cw_ref[0:1, :]
    for tap in range(1, CONV_W):
        lo = CONV_PAD - (CONV_W - 1) + tap
        conv = conv + xpad[lo:lo + L, :] * cw_ref[tap:tap + 1, :]
    xpad[0:CONV_PAD, :] = xpad[L:L + CONV_PAD, :]
    xa = _silu(conv)
    xs = xa[:, :d_inner]
    bm = xa[:, d_inner:d_inner + gn]
    cm = xa[:, d_inner + gn:]

    dt = _softplus(dt_ref[0] + dtb_ref[...])
    a = -jnp.exp(alog_ref[...])
    row = lax.broadcasted_iota(jnp.int32, (L, L), 0)
    col = lax.broadcasted_iota(jnp.int32, (L, L), 1)
    causal = row >= col
    acs = _dot_sel_l(causal.astype(BF16), dt * a)
    acs_t = acs.T
    dt_t = dt.T
    acs_last = acs[L - 1:L, :]

    n_heads = d_inner // HEAD_DIM
    expand = _head_expand(LANES, n_heads)
    e_acs = _dot(jnp.exp(acs).astype(BF16), expand)
    w_end = _dot((jnp.exp(acs_last - acs) * dt).astype(BF16), expand)
    dec_state = _dot_sel_r(jnp.broadcast_to(jnp.exp(acs_last), (SUBLANES, LANES)), expand)[0:1, :]

    xs_b = xs.astype(BF16)
    xw = (xs * w_end).astype(BF16)
    lane_first = lax.broadcasted_iota(jnp.int32, (L, LANES), 1) < HEAD_DIM
    y_groups = []
    for g in range(SSM_GROUPS):
        gsl = slice(g * gw, (g + 1) * gw)
        b_g = bm[:, g * SSM_STATE:(g + 1) * SSM_STATE]
        c_g = cm[:, g * SSM_STATE:(g + 1) * SSM_STATE].astype(BF16)
        cb = _dot_nt(c_g, b_g.astype(BF16))
        st_g = st_t[g]
        y_inter = _dot(c_g, st_g.astype(BF16))
        pairs = []
        for pr in range(HPG // 2):
            res = []
            for sub in range(2):
                h = g * HPG + pr * 2 + sub
                seg = acs[:, h:h + 1] - acs_t[h:h + 1, :]
                w = cb * jnp.exp(jnp.where(causal, seg, -jnp.inf)) * dt_t[h:h + 1, :]
                lo = g * gw + pr * LANES
                res.append(_dot(w.astype(BF16), xs_b[:, lo:lo + LANES]))
            pairs.append(jnp.where(lane_first, res[0], res[1]))
        y_groups.append(jnp.concatenate(pairs, axis=1) + y_inter * e_acs[:, gsl])
        st_t[g] = dec_state[:, gsl] * st_g + _dot(b_g.T.astype(BF16), xw[:, gsl])
    y = jnp.concatenate(y_groups, axis=1) + dskip_ref[...] * xs
    y_ref[0] = _ssd_gate_norm(y, z_ref[0].astype(F32), ng_ref[...]).astype(y_ref.dtype)

    @pl.when(c == pl.num_programs(1) - 1)
    def _():
        for g in range(SSM_GROUPS):
            st_ref[0, g * HPG:(g + 1) * HPG] = st_t[g].T.reshape(HPG, HEAD_DIM, SSM_STATE)


def _ssd_prompt(z, xbc, dt, cw, cb, dtb, alog, dskip, ng):
    b, s, d_inner = z.shape
    conv_dim = xbc.shape[-1]
    n_heads = d_inner // HEAD_DIM
    L = SSD_CHUNK
    blk = lambda n: pl.BlockSpec((1, L, n), lambda i, c: (i, c, 0))
    vmem = (2 * (_nbytes((L, d_inner), z.dtype) * 2 + _nbytes((L, conv_dim), xbc.dtype) + _nbytes((L, LANES), F32))
            + 2 * _nbytes((n_heads, HEAD_DIM, SSM_STATE), F32)
            + _nbytes((CONV_PAD + L, conv_dim), F32) + _nbytes((SSM_GROUPS, SSM_STATE, HPG * HEAD_DIM), F32)
            + 16 * _nbytes((L, conv_dim), F32))
    return pl.pallas_call(
        _ssd_prompt_kernel,
        grid=(b, s // L),
        in_specs=[blk(d_inner), blk(conv_dim), blk(LANES),
                  _resident(cw.shape), _resident(cb.shape), _resident(dtb.shape), _resident(alog.shape),
                  _resident(dskip.shape), _resident(ng.shape)],
        out_specs=[blk(d_inner),
                   pl.BlockSpec((1, n_heads, HEAD_DIM, SSM_STATE), lambda i, c: (i, 0, 0, 0))],
        out_shape=[jax.ShapeDtypeStruct((b, s, d_inner), BF16),
                   jax.ShapeDtypeStruct((b, n_heads, HEAD_DIM, SSM_STATE), F32)],
        scratch_shapes=[pltpu.VMEM((CONV_PAD + L, conv_dim), F32),
                        pltpu.VMEM((SSM_GROUPS, SSM_STATE, HPG * HEAD_DIM), F32)],
        compiler_params=_cparams(("parallel", "arbitrary"), vmem),
        name="ssd_prompt_scan",
    )(z, xbc, dt, cw, cb, dtb, alog, dskip, ng)


def _attn_prompt_kernel(*refs, first, last):
    q_ref, kp_ref, kc_ref, vp_ref, vc_ref = refs[:5]
    refs = refs[5:]
    if not first:
        oin_ref, lin_ref = refs[:2]
        refs = refs[2:]
    o_ref = refs[0]
    n = pl.program_id(2)
    gw = HPG * HEAD_DIM
    q = q_ref[0] * ATTN_SCALE
    kk = jnp.concatenate([kp_ref[0], kc_ref[0]], axis=0)
    vv = jnp.concatenate([vp_ref[0], vc_ref[0]], axis=0)
    qi = lax.broadcasted_iota(jnp.int32, (NK, 2 * NK), 0)
    kj = lax.broadcasted_iota(jnp.int32, (NK, 2 * NK), 1)
    dist = qi + NK - kj
    mask = (dist >= 0) & (dist <= NK) & ((kj >= NK) | (n > 0))
    lane = lax.broadcasted_iota(jnp.int32, (NK, LANES), 1)
    lane_first = lane < HEAD_DIM
    lse_out = jnp.zeros((NK, LANES), F32)
    o_parts = []
    for pr in range(HPG // 2):
        sl = slice(pr * LANES, (pr + 1) * LANES)
        q_pair, k_pair, v_pair = q[:, sl], kk[:, sl], vv[:, sl]
        res = []
        for sub in range(2):
            h = pr * 2 + sub
            qm = jnp.where(lane_first if sub == 0 else ~lane_first, q_pair, jnp.zeros_like(q_pair))
            s = jnp.where(mask, _dot_nt(qm, k_pair), -jnp.inf)
            m = jnp.max(s, axis=-1, keepdims=True)
            p = jnp.exp(s - m)
            l = jnp.sum(p, axis=-1, keepdims=True)
            o = _dot(p.astype(BF16), v_pair) / l
            lse = m + jnp.log(l)
            if not first:
                lse_prev = lin_ref[0][:, h:h + 1]
                lse_new = jnp.maximum(lse, lse_prev) + jnp.log1p(jnp.exp(-jnp.abs(lse - lse_prev)))
                o = (oin_ref[0][:, sl].astype(F32) * jnp.exp(lse_prev - lse_new) + o * jnp.exp(lse - lse_new))
                lse = lse_new
            res.append(o)
            lse_out = jnp.where(lane == h, lse, lse_out)
        o_parts.append(jnp.where(lane_first, res[0], res[1]))
    o_ref[0] = jnp.concatenate(o_parts, axis=1).astype(o_ref.dtype)
    if not last:
        refs[1][0] = lse_out


def _attn_prompt_group(q, k, v, gi, o_in, lse_in):
    b, s, width = q.shape
    gw = HPG * HEAD_DIM
    n_grp = width // gw
    window, dil = DIL_GROUPS[gi]
    assert window // dil == NK and s % window == 0
    first, last = gi == 0, gi == len(DIL_GROUPS) - 1
    sd = s // dil
    nb = sd // NK
    view = lambda t: t.reshape(b, sd, dil * t.shape[-1])
    cur = pl.BlockSpec((1, NK, gw), lambda i, r, n: (i, n, r * n_grp + gi))
    prev = pl.BlockSpec((1, NK, gw), lambda i, r, n: (i, jnp.maximum(n - 1, 0), r * n_grp + gi))
    o_spec = pl.BlockSpec((1, NK, gw), lambda i, r, n: (i, n, r))
    l_spec = pl.BlockSpec((1, NK, LANES), lambda i, r, n: (i, n, r))
    args = [view(q), view(k), view(k), view(v), view(v)]
    in_specs = [cur, prev, cur, prev, cur]
    if not first:
        args += [view(o_in), view(lse_in)]
        in_specs += [o_spec, l_spec]
    out_specs = [o_spec]
    out_shape = [jax.ShapeDtypeStruct((b, sd, dil * gw), BF16)]
    if not last:
        out_specs.append(l_spec)
        out_shape.append(jax.ShapeDtypeStruct((b, sd, dil * LANES), F32))
    vmem = 2 * 7 * _nbytes((NK, gw), BF16) + 4 * _nbytes((NK, LANES), F32) + 24 * _nbytes((NK, 2 * NK), F32)
    outs = pl.pallas_call(
        functools.partial(_attn_prompt_kernel, first=first, last=last),
        grid=(b, dil, nb),
        in_specs=in_specs, out_specs=out_specs, out_shape=out_shape,
        compiler_params=_cparams(("parallel", "parallel", "arbitrary"), vmem),
        name=f"attn_prompt_w{window}",
    )(*args)
    o = outs[0].reshape(b, s, gw)
    lse = None if last else outs[1].reshape(b, s, LANES)
    return o, lse


def _out_kernel(h_ref, y_ref, at_ref, gs_ref, ga_ref, p_ref,
                wos_ref, woa_ref, wo_ref, n2_ref, wg_ref, wu_ref, wd_ref, np_ref, wpg_ref, wpp_ref, nf_ref,
                o_ref):
    merged = (gs_ref[...].astype(F32) * _dot(y_ref[...].astype(BF16), wos_ref[...])
              + ga_ref[...].astype(F32) * _dot(at_ref[...].astype(BF16), woa_ref[...]))
    h = h_ref[...] + _dot(merged.astype(BF16), wo_ref[...])
    h = _swiglu_half(h, n2_ref[...], wg_ref, wu_ref, wd_ref)
    gate = _sigmoid(_dot(_rms(h, np_ref[...]).astype(BF16), wpg_ref[...]))
    h = h + gate * _dot(p_ref[...].astype(BF16), wpp_ref[...])
    o_ref[...] = _rms(h, nf_ref[...])


def _out_stage(h, y, attn, gs, ga, p, weights, tm):
    t, d = h.shape
    row = lambda i: (i, 0)
    acts = [h, y, attn, gs, ga, p]
    vmem = (sum(_nbytes(w.shape, w.dtype) for w in weights)
            + sum(2 * _nbytes((tm, a.shape[1]), a.dtype) for a in acts) + 2 * _nbytes((tm, d), F32)
            + 8 * _nbytes((tm, d), F32) + 4 * _nbytes((tm, FF_CHUNK), F32))
    return pl.pallas_call(
        _out_kernel,
        grid=(t // tm,),
        in_specs=[pl.BlockSpec((tm, a.shape[1]), row) for a in acts] + [_resident(w.shape) for w in weights],
        out_specs=pl.BlockSpec((tm, d), row),
        out_shape=jax.ShapeDtypeStruct((t, d), F32),
        compiler_params=_cparams(("parallel",), vmem),
        name="output_stage",
    )(*acts, *weights)


def _ssd_sample_prep_kernel(xn_ref, sc_ref, dtr_ref, cw_ref, cb_ref, dtb_ref, alog_ref,
                            xs_ref, xdt_t_ref, bdt_ref, c_t_ref, dec_ref):
    conv_dim = xn_ref.shape[1]
    d_inner = xs_ref.shape[1]
    gn = SSM_GROUPS * SSM_STATE
    conv = cb_ref[...] + xn_ref[...] * cw_ref[CONV_W - 1:CONV_W, :]
    for tap in range(CONV_W - 1):
        conv = conv + sc_ref[:, tap * conv_dim:(tap + 1) * conv_dim] * cw_ref[tap:tap + 1, :]
    xa = _silu(conv)
    xs = xa[:, :d_inner]
    xs_ref[...] = xs
    dt = _softplus(dtr_ref[...] + dtb_ref[...])
    dec_ref[...] = jnp.exp(dt * -jnp.exp(alog_ref[...]))
    n_heads = d_inner // HEAD_DIM
    dt_x = _dot_sel_r(dt, _head_expand(LANES, n_heads))
    xdt_t_ref[...] = (xs * dt_x).T.astype(xdt_t_ref.dtype)
    bdt_ref[...] = xa[:, d_inner:d_inner + gn]
    for g in range(SSM_GROUPS):
        lo = d_inner + gn + g * SSM_STATE
        c_t_ref[g] = xa[:, lo:lo + SSM_STATE].T.astype(c_t_ref.dtype)


def _ssd_sample_prep(xn, sc, dtr, cw, cb, dtb, alog, d_inner):
    nb, conv_dim = xn.shape
    args = [xn, sc, dtr, cw, cb, dtb, alog]
    out_shape = [jax.ShapeDtypeStruct((nb, d_inner), F32),
                 jax.ShapeDtypeStruct((d_inner, nb), BF16),
                 jax.ShapeDtypeStruct((nb, SSM_GROUPS * SSM_STATE), F32),
                 jax.ShapeDtypeStruct((SSM_GROUPS, SSM_STATE, nb), BF16),
                 jax.ShapeDtypeStruct((nb, LANES), F32)]
    vmem = 2 * sum(_nbytes(a.shape, a.dtype) for a in args) + 2 * sum(_nbytes(o.shape, o.dtype) for o in out_shape) \
        + 8 * _nbytes((nb, conv_dim), F32)
    return pl.pallas_call(
        _ssd_sample_prep_kernel,
        grid=(1,),
        in_specs=[_resident(a.shape) for a in args],
        out_specs=[pl.BlockSpec(o.shape, functools.partial(lambda nd, i: (0,) * nd, len(o.shape))) for o in out_shape],
        out_shape=out_shape,
        compiler_params=_cparams(("arbitrary",), vmem),
        name="ssd_sample_prep",
    )(*args)


SAMPLE_STATE_TILE = 4


def _ssd_sample_state_kernel(dec_ref, st_ref, xdt_t_ref, bdt_ref, c_t_ref, sto_ref, y_t_ref):
    i = pl.program_id(0)
    nb = bdt_ref.shape[0]
    n_heads = st_ref.shape[1]
    gw = HPG * HEAD_DIM

    @pl.when(i == 0)
    def _():
        y_t_ref[...] = jnp.zeros(y_t_ref.shape, F32)

    row = lax.broadcasted_iota(jnp.int32, (nb, SSM_STATE), 0)
    lane = lax.broadcasted_iota(jnp.int32, (gw, nb), 1)
    for j in range(SAMPLE_STATE_TILE):
        b = i * SAMPLE_STATE_TILE + j
        for g in range(SSM_GROUPS):
            b_row = jnp.where(row == b, bdt_ref[:, g * SSM_STATE:(g + 1) * SSM_STATE], 0.0).astype(BF16)
            upd = _dot(xdt_t_ref[g * gw:(g + 1) * gw, :], b_row)
            new = []
            for hg in range(HPG):
                h = g * HPG + hg
                new_h = st_ref[j, h] * dec_ref[b * n_heads + h] + upd[hg * HEAD_DIM:(hg + 1) * HEAD_DIM, :]
                sto_ref[j, h] = new_h
                new.append(new_h)
            yg = _dot(jnp.concatenate(new, axis=0).astype(BF16), c_t_ref[g])
            y_t_ref[g * gw:(g + 1) * gw, :] += jnp.where(lane == b, yg, 0.0)


def _ssd_sample_state(dec, st, xdt_t, bdt, c_t):
    nb, n_heads = st.shape[0], st.shape[1]
    d_inner = n_heads * HEAD_DIM
    bt = SAMPLE_STATE_TILE
    st_spec = pl.BlockSpec((bt, n_heads, HEAD_DIM, SSM_STATE), lambda i: (i, 0, 0, 0))
    vmem = (4 * _nbytes((bt, n_heads, HEAD_DIM, SSM_STATE), F32) + _nbytes(xdt_t.shape, BF16)
            + _nbytes(bdt.shape, F32) + _nbytes(c_t.shape, BF16) + 2 * _nbytes((d_inner, nb), F32)
            + 16 * _nbytes((HPG * HEAD_DIM, SSM_STATE), F32))
    return pl.pallas_call(
        _ssd_sample_state_kernel,
        grid=(nb // bt,),
        in_specs=[pl.BlockSpec(memory_space=pltpu.SMEM), st_spec,
                  _resident(xdt_t.shape), _resident(bdt.shape), _resident(c_t.shape)],
        out_specs=[st_spec, pl.BlockSpec((d_inner, nb), lambda i: (0, 0))],
        out_shape=[jax.ShapeDtypeStruct(st.shape, F32), jax.ShapeDtypeStruct((d_inner, nb), F32)],
        compiler_params=_cparams(("arbitrary",), vmem),
        name="ssd_sample_state",
    )(dec, st, xdt_t, bdt, c_t)


def _ssd_sample_post_kernel(y_t_ref, xs_ref, z_ref, dskip_ref, ng_ref, y_ref):
    y = y_t_ref[...].T + dskip_ref[...] * xs_ref[...]
    y_ref[...] = _ssd_gate_norm(y, z_ref[...], ng_ref[...]).astype(y_ref.dtype)


def _ssd_sample_post(y_t, xs, z, dskip, ng):
    args = [y_t, xs, z, dskip, ng]
    vmem = 2 * sum(_nbytes(a.shape, a.dtype) for a in args) + 8 * _nbytes(xs.shape, F32)
    return pl.pallas_call(
        _ssd_sample_post_kernel,
        grid=(1,),
        in_specs=[_resident(a.shape) for a in args],
        out_specs=pl.BlockSpec(xs.shape, lambda i: (0, 0)),
        out_shape=jax.ShapeDtypeStruct(xs.shape, F32),
        compiler_params=_cparams(("arbitrary",), vmem),
        name="ssd_sample_post",
    )(*args)


def _attn_sample_kernel(q_ref, k_ref, v_ref, c0_ref, c1_ref, c2_ref, o_ref):
    gw = HPG * HEAD_DIM
    collapse = _head_collapse(HPG, LANES)
    expand = _head_expand(LANES, HPG)
    scores, values = [], []
    for gi, c_ref in enumerate((c0_ref, c1_ref, c2_ref)):
        sl = slice(gi * gw, (gi + 1) * gw)
        qg = q_ref[0][:, sl] * ATTN_SCALE
        keys = jnp.concatenate([c_ref[0][:, :gw], jnp.broadcast_to(k_ref[0][:, sl], (SUBLANES, gw))], axis=0)
        values.append((c_ref[0][:, gw:], v_ref[0][:, sl]))
        scores.append(_dot_sel_r(keys * qg, collapse))
    m = functools.reduce(jnp.maximum, [jnp.max(s, axis=0, keepdims=True) for s in scores])
    num = jnp.zeros((1, gw), F32)
    den = jnp.zeros((1, LANES), F32)
    for s, (vc, vn) in zip(scores, values):
        p = jnp.exp(s - m)
        p_c, p_n = p[:NK], p[NK:NK + 1]
        den = den + jnp.sum(p_c, axis=0, keepdims=True) + p_n
        pe = _dot_sel_r(p[:NK + SUBLANES], expand)
        num = num + jnp.sum(pe[:NK] * vc, axis=0, keepdims=True) + pe[NK:NK + 1] * vn
    den_e = _dot_sel_r(jnp.broadcast_to(den, (SUBLANES, LANES)), expand)[0:1]
    o_ref[0] = num / den_e


def _attn_sample(q, k, v, caches):
    nb = q.shape[0]
    gw = HPG * HEAD_DIM
    views, specs = [], []
    for (window, dil), c in zip(DIL_GROUPS, caches):
        assert c.shape[1] == window and window // dil == NK
        views.append(c.reshape(nb, NK, dil * 2 * gw))
        specs.append(pl.BlockSpec((1, NK, 2 * gw), lambda i: (i, 0, 0)))
    vec = pl.BlockSpec((1, 1, q.shape[-1]), lambda i: (i, 0, 0))
    vmem = 2 * 3 * _nbytes((NK, 2 * gw), F32) + 16 * _nbytes((NK + SUBLANES, gw), F32)
    return pl.pallas_call(
        _attn_sample_kernel,
        grid=(nb,),
        in_specs=[vec, vec, vec] + specs,
        out_specs=pl.BlockSpec((1, 1, gw), lambda i: (i, 0, 0)),
        out_shape=jax.ShapeDtypeStruct((nb, 1, gw), F32),
        compiler_params=_cparams(("parallel",), vmem),
        name="attn_sample",
    )(q, k, v, *views)


def _row(v, width=None):
    v = v.astype(F32).reshape(1, -1)
    if width is not None and v.shape[1] < width:
        v = jnp.pad(v, ((0, 0), (0, width - v.shape[1])))
    return v


def _layer_weights(prm):
    d_model = prm['w_in'].shape[0]
    d_inner = prm['norm_ssm'].shape[0]
    n_heads = d_inner // HEAD_DIM
    conv_dim = prm['conv_w'].shape[1]
    attn_w = len(DIL_GROUPS) * HPG * HEAD_DIM
    splits = [d_inner, conv_dim, n_heads, attn_w, attn_w, attn_w, d_model, d_model]
    offs = [0]
    for n in splits:
        offs.append(offs[-1] + n)
    w_in = prm['w_in']
    seg = [w_in[:, offs[j]:offs[j + 1]].astype(BF16) for j in range(len(splits))]
    seg[2] = jnp.pad(seg[2], ((0, 0), (0, LANES - n_heads)))
    bf = lambda name: prm[name].astype(BF16)
    return dict(
        ffn1=(_row(prm['norm_ffn1']), bf('w_ffn1_gate'), bf('w_ffn1_up'), bf('w_ffn1_down'), _row(prm['norm_mix'])),
        proj_ssm=seg[0:3], proj_attn=seg[3:8],
        conv_w=prm['conv_w'].astype(F32), conv_b=_row(prm['conv_b']),
        dt_bias=_row(prm['dt_bias'], LANES), a_log=_row(prm['a_log'], LANES),
        d_skip=_row(jnp.repeat(prm['d_skip'], HEAD_DIM)), norm_ssm=_row(prm['norm_ssm']),
        out=(bf('w_o_ssm'), bf('w_o_attn'), bf('w_out'), _row(prm['norm_ffn2']), bf('w_ffn2_gate'),
             bf('w_ffn2_up'), bf('w_ffn2_down'), _row(prm['norm_ple']), bf('w_ple_gate'), bf('w_ple_proj')),
        d_inner=d_inner, conv_dim=conv_dim,
    )


def _kv_stack(k, v, gi, keep):
    b, s, _ = k.shape
    gw = HPG * HEAD_DIM
    sel = lambda t: t[:, s - keep:, gi * gw:(gi + 1) * gw].astype(F32).reshape(b, keep, HPG, HEAD_DIM)
    return jnp.stack([sel(k), sel(v)], axis=2)


def _prompt_layer(x, p_emb, w, norm_final, tm):
    b, s, d = x.shape
    t = b * s
    h1, u = _ffn(x.reshape(t, d), *w['ffn1'], tm)
    z, xbc, dt = _proj(_proj_ssm_kernel, u, w['proj_ssm'], (BF16, BF16, F32), tm, "proj_ssm")
    q, k, v, gs, ga = _proj(functools.partial(_proj_attn_kernel, seq=s, pos0=0), u, w['proj_attn'],
                            (BF16,) * 5, tm, "proj_attn")
    r3 = lambda a: a.reshape(b, s, a.shape[-1])
    xbc3 = r3(xbc)
    y_ssm, ssm_new = _ssd_prompt(r3(z), xbc3, r3(dt), w['conv_w'], w['conv_b'], w['dt_bias'], w['a_log'],
                                 w['d_skip'], w['norm_ssm'])
    q3, k3, v3 = r3(q), r3(k), r3(v)
    o, lse = None, None
    for gi in range(len(DIL_GROUPS)):
        o, lse = _attn_prompt_group(q3, k3, v3, gi, o, lse)
    y = _out_stage(h1, y_ssm.reshape(t, -1), o.reshape(t, -1), gs, ga, p_emb.reshape(t, -1),
                   w['out'] + (_row(norm_final),), OUT_ROW_TILE)
    kv = [_kv_stack(k3, v3, gi, min(window, s)) for gi, (window, _) in enumerate(DIL_GROUPS)]
    conv_new = xbc3[:, s - (CONV_W - 1):, :].astype(F32)
    return y.reshape(b, s, d), kv, conv_new, ssm_new


def _sample_layer(x, p_emb, w, norm_final, conv_prev, ssm_prev, caches):
    nb, s, d = x.shape
    assert s == 1
    h1, u = _ffn(x.reshape(nb, d), *w['ffn1'], nb)
    z, xbc, dt = _proj(_proj_ssm_kernel, u, w['proj_ssm'], (F32, F32, F32), nb, "proj_ssm_sample")
    q, k, v, gs, ga = _proj(functools.partial(_proj_attn_kernel, seq=1, pos0=PAST_LEN), u, w['proj_attn'],
                            (F32,) * 5, nb, "proj_attn_sample")
    xs, xdt_t, bdt, c_t, dec = _ssd_sample_prep(xbc, conv_prev.reshape(nb, -1), dt, w['conv_w'], w['conv_b'],
                                                w['dt_bias'], w['a_log'], w['d_inner'])
    dec_flat = dec[:, :ssm_prev.shape[1]].reshape(-1)
    ssm_new, y_t = _ssd_sample_state(dec_flat, ssm_prev, xdt_t, bdt, c_t)
    y_ssm = _ssd_sample_post(y_t, xs, z, w['d_skip'], w['norm_ssm'])
    r3 = lambda a: a.reshape(nb, 1, a.shape[-1])
    attn = _attn_sample(r3(q), r3(k), r3(v), caches)
    y = _out_stage(h1, y_ssm, attn.reshape(nb, -1), gs, ga, p_emb.reshape(nb, -1),
                   w['out'] + (_row(norm_final),), nb)
    kv = [_kv_stack(r3(k), r3(v), gi, 1) for gi in range(len(DIL_GROUPS))]
    conv_new = jnp.concatenate([conv_prev[:, 1:], xbc[:, None, :]], axis=1)
    return y.reshape(nb, 1, d), kv, conv_new, ssm_new


PROMPT_ROW_TILE = 512
OUT_ROW_TILE = 256


def kernel(x_prompt, x_sample, cache_kv_w128, cache_kv_w512, cache_kv_w2048, state_conv, state_ssm, p_prompt, p_sample, norm_ffn1, w_ffn1_gate, w_ffn1_up, w_ffn1_down, norm_mix, w_in, conv_w, conv_b, dt_bias, a_log, d_skip, norm_ssm, w_o_ssm, w_o_attn, w_out, norm_ffn2, w_ffn2_gate, w_ffn2_up, w_ffn2_down, norm_ple, w_ple_gate, w_ple_proj, norm_final):
    depth = w_in.shape[0]
    assert depth == 1
    layer_params = dict(
        norm_ffn1=norm_ffn1, w_ffn1_gate=w_ffn1_gate, w_ffn1_up=w_ffn1_up, w_ffn1_down=w_ffn1_down,
        norm_mix=norm_mix, w_in=w_in, conv_w=conv_w, conv_b=conv_b, dt_bias=dt_bias, a_log=a_log,
        d_skip=d_skip, norm_ssm=norm_ssm, w_o_ssm=w_o_ssm, w_o_attn=w_o_attn, w_out=w_out,
        norm_ffn2=norm_ffn2, w_ffn2_gate=w_ffn2_gate, w_ffn2_up=w_ffn2_up, w_ffn2_down=w_ffn2_down,
        norm_ple=norm_ple, w_ple_gate=w_ple_gate, w_ple_proj=w_ple_proj)
    i = 0
    w = _layer_weights({name: val[i] for name, val in layer_params.items()})
    yp, kvp, convp, ssmp = _prompt_layer(x_prompt, p_prompt[i], w, norm_final, PROMPT_ROW_TILE)
    ys, kvs, convs, ssms = _sample_layer(x_sample, p_sample[i], w, norm_final, state_conv[i], state_ssm[i],
                                         (cache_kv_w128[i], cache_kv_w512[i], cache_kv_w2048[i]))
    st = lambda a: a[None]
    return (yp, ys, st(kvp[0]), st(kvp[1]), st(kvp[2]), st(convp), st(ssmp),
            st(kvs[0]), st(kvs[1]), st(kvs[2]), st(convs), st(ssms))
```

```python
import functools
import math

import jax
import jax.numpy as jnp
from jax import lax
from jax.experimental import pallas as pl
from jax.experimental.pallas import tpu as pltpu

F32 = jnp.float32
BF16 = jnp.bfloat16

EPS = 1e-6
ROPE_THETA = 10000.0
PAST_LEN = 8192
HEAD_DIM = 64
HPG = 8
SSM_GROUPS = 4
SSM_STATE = 128
CONV_W = 4
DIL_GROUPS = ((128, 1), (512, 4), (2048, 16))
ATTN_SCALE = HEAD_DIM ** -0.5
NK = 128

LANES = 128
SUBLANES = 8
VMEM_LIMIT_CAP = 56 * 1024 * 1024


def _cparams(sem, vmem_bytes):
    return pltpu.CompilerParams(dimension_semantics=sem,
                                vmem_limit_bytes=int(min(VMEM_LIMIT_CAP, vmem_bytes)))


def _resident(shape):
    nd = len(shape)
    return pl.BlockSpec(shape, lambda *_: (0,) * nd, pipeline_mode=pl.Buffered(1))


def _nbytes(shape, dtype):
    return math.prod(shape) * jnp.dtype(dtype).itemsize


def _rms(x, gain):
    ms = jnp.mean(x * x, axis=-1, keepdims=True)
    return x * lax.rsqrt(ms + EPS) * gain


def _sigmoid(x):
    return 1.0 / (1.0 + jnp.exp(-x))


def _silu(x):
    return x * _sigmoid(x)


def _softplus(x):
    return jnp.maximum(x, 0.0) + jnp.log1p(jnp.exp(-jnp.abs(x)))


def _dot(a, b):
    return jnp.dot(a, b, preferred_element_type=F32)


def _dot_nt(a, b):
    return lax.dot_general(a, b, (((1,), (1,)), ((), ())), preferred_element_type=F32)


def _split3(x):
    hi = x.astype(BF16)
    r1 = x - hi.astype(F32)
    mid = r1.astype(BF16)
    lo = (r1 - mid.astype(F32)).astype(BF16)
    return hi, mid, lo


def _dot_sel_l(sel, x):
    hi, mid, lo = _split3(x)
    return _dot(sel, hi) + _dot(sel, mid) + _dot(sel, lo)


def _dot_sel_r(x, sel):
    hi, mid, lo = _split3(x)
    return _dot(hi, sel) + _dot(mid, sel) + _dot(lo, sel)


def _head_expand(n_heads_padded, n_heads):
    r = lax.broadcasted_iota(jnp.int32, (n_heads_padded, n_heads * HEAD_DIM), 0)
    c = lax.broadcasted_iota(jnp.int32, (n_heads_padded, n_heads * HEAD_DIM), 1)
    return (c // HEAD_DIM == r).astype(BF16)


FF_CHUNK = 256


def _swiglu_half(x, gain, wg_ref, wu_ref, wd_ref):
    u = _rms(x, gain).astype(BF16)
    d_ff = wg_ref.shape[1]
    acc = jnp.zeros(x.shape, F32)
    for c in range(d_ff // FF_CHUNK):
        sl = slice(c * FF_CHUNK, (c + 1) * FF_CHUNK)
        g = _dot(u, wg_ref[:, sl])
        up = _dot(u, wu_ref[:, sl])
        acc = acc + _dot((_silu(g) * up).astype(BF16), wd_ref[sl, :])
    return x + 0.5 * acc


def _ffn_kernel(x_ref, n1_ref, wg_ref, wu_ref, wd_ref, n2_ref, h_ref, u_ref):
    h = _swiglu_half(x_ref[...], n1_ref[...], wg_ref, wu_ref, wd_ref)
    h_ref[...] = h
    u_ref[...] = _rms(h, n2_ref[...]).astype(u_ref.dtype)


def _ffn(x, n1, wg, wu, wd, n2, tm):
    t, d = x.shape
    d_ff = wg.shape[1]
    vmem = (3 * _nbytes((d, d_ff), BF16) + 2 * 2 * _nbytes((tm, d), F32) + 2 * _nbytes((tm, d), BF16)
            + 6 * _nbytes((tm, d), F32) + 4 * _nbytes((tm, FF_CHUNK), F32))
    return pl.pallas_call(
        _ffn_kernel,
        grid=(t // tm,),
        in_specs=[pl.BlockSpec((tm, d), lambda i: (i, 0)),
                  _resident((1, d)), _resident((d, d_ff)), _resident((d, d_ff)), _resident((d_ff, d)),
                  _resident((1, d))],
        out_specs=[pl.BlockSpec((tm, d), lambda i: (i, 0)), pl.BlockSpec((tm, d), lambda i: (i, 0))],
        out_shape=[jax.ShapeDtypeStruct((t, d), F32), jax.ShapeDtypeStruct((t, d), BF16)],
        compiler_params=_cparams(("parallel",), vmem),
        name="ffn_half_step",
    )(x, n1, wg, wu, wd, n2)


def _rope_tables(tm, seq, pos0):
    row = lax.broadcasted_iota(jnp.int32, (tm, LANES), 0) + pl.program_id(0) * tm
    pos = (row % seq + pos0).astype(F32)
    lane = lax.broadcasted_iota(jnp.int32, (tm, LANES), 1)
    half = HEAD_DIM // 2
    j = (lane % half).astype(F32)
    inv_freq = jnp.exp(j * (-math.log(ROPE_THETA) / half))
    ang = pos * inv_freq
    first = (lane % HEAD_DIM) < half
    return jnp.cos(ang), jnp.where(first, -jnp.sin(ang), jnp.sin(ang)), first


def _rotary(x, cos, sin_signed, first):
    outs = []
    for c in range(x.shape[1] // LANES):
        xb = x[:, c * LANES:(c + 1) * LANES]
        other = jnp.where(first, pltpu.roll(xb, LANES - HEAD_DIM // 2, 1), pltpu.roll(xb, HEAD_DIM // 2, 1))
        outs.append(xb * cos + other * sin_signed)
    return jnp.concatenate(outs, axis=1)


def _proj_ssm_kernel(u_ref, wz_ref, wx_ref, wdt_ref, z_ref, xbc_ref, dt_ref):
    u = u_ref[...]
    z_ref[...] = _dot(u, wz_ref[...]).astype(z_ref.dtype)
    xbc_ref[...] = _dot(u, wx_ref[...]).astype(xbc_ref.dtype)
    dt_ref[...] = _dot(u, wdt_ref[...])


def _proj_attn_kernel(u_ref, wq_ref, wk_ref, wv_ref, wgs_ref, wga_ref,
                      q_ref, k_ref, v_ref, gs_ref, ga_ref, *, seq, pos0):
    u = u_ref[...]
    cos, sin_signed, first = _rope_tables(u.shape[0], seq, pos0)
    q_ref[...] = _rotary(_dot(u, wq_ref[...]), cos, sin_signed, first).astype(q_ref.dtype)
    k_ref[...] = _rotary(_dot(u, wk_ref[...]), cos, sin_signed, first).astype(k_ref.dtype)
    v_ref[...] = _dot(u, wv_ref[...]).astype(v_ref.dtype)
    gs_ref[...] = _sigmoid(_dot(u, wgs_ref[...])).astype(gs_ref.dtype)
    ga_ref[...] = _sigmoid(_dot(u, wga_ref[...])).astype(ga_ref.dtype)


GROUP_WIDTH = HPG * HEAD_DIM


def _store_wide(stage, val, out_ref, dil):
    if dil == 1:
        out_ref[...] = val.astype(out_ref.dtype)
        return
    rows = val.shape[0] // dil
    for j in range(GROUP_WIDTH // LANES):
        stage[j] = val[:, j * LANES:(j + 1) * LANES]
    for r in range(dil):
        for j in range(GROUP_WIDTH // LANES):
            lo = r * GROUP_WIDTH + j * LANES
            out_ref[:, lo:lo + LANES] = stage[j, pl.ds(r, rows, stride=dil), :].astype(out_ref.dtype)


def _proj_attn_wide_kernel(u_ref, wq_ref, wk_ref, wv_ref, wgs_ref, wga_ref, *refs, seq):
    n_grp = len(DIL_GROUPS)
    q_refs, k_refs, v_refs = refs[0:n_grp], refs[n_grp:2 * n_grp], refs[2 * n_grp:3 * n_grp]
    gs_ref, ga_ref, stage = refs[3 * n_grp:]
    u = u_ref[...]
    cos, sin_signed, first = _rope_tables(u.shape[0], seq, 0)
    for w_ref, out_refs, rope in ((wq_ref, q_refs, True), (wk_ref, k_refs, True), (wv_ref, v_refs, False)):
        res = _dot(u, w_ref[...])
        if rope:
            res = _rotary(res, cos, sin_signed, first)
        for gi, (_, dil) in enumerate(DIL_GROUPS):
            _store_wide(stage, res[:, gi * GROUP_WIDTH:(gi + 1) * GROUP_WIDTH], out_refs[gi], dil)
    gs_ref[...] = _sigmoid(_dot(u, wgs_ref[...])).astype(gs_ref.dtype)
    ga_ref[...] = _sigmoid(_dot(u, wga_ref[...])).astype(ga_ref.dtype)


def _proj_attn_wide(u, weights, seq, tm):
    t, d = u.shape
    d_model = weights[3].shape[1]
    row = lambda i: (i, 0)
    qkv_specs = [pl.BlockSpec((tm // dil, dil * GROUP_WIDTH), row) for _, dil in DIL_GROUPS] * 3
    qkv_shapes = [jax.ShapeDtypeStruct((t // dil, dil * GROUP_WIDTH), BF16) for _, dil in DIL_GROUPS] * 3
    vmem = (sum(_nbytes(w.shape, BF16) for w in weights) + 2 * _nbytes((tm, d), BF16)
            + 2 * _nbytes((tm, 3 * len(DIL_GROUPS) * GROUP_WIDTH + 2 * d_model), BF16)
            + _nbytes((tm, GROUP_WIDTH), F32) + 6 * _nbytes((tm, weights[0].shape[1]), F32))
    outs = pl.pallas_call(
        functools.partial(_proj_attn_wide_kernel, seq=seq),
        grid=(t // tm,),
        in_specs=[pl.BlockSpec((tm, d), row)] + [_resident(w.shape) for w in weights],
        out_specs=qkv_specs + [pl.BlockSpec((tm, d_model), row)] * 2,
        out_shape=qkv_shapes + [jax.ShapeDtypeStruct((t, d_model), BF16)] * 2,
        scratch_shapes=[pltpu.VMEM((GROUP_WIDTH // LANES, tm, LANES), F32)],
        compiler_params=_cparams(("parallel",), vmem),
        name="proj_attn",
    )(u, *weights)
    n_grp = len(DIL_GROUPS)
    return outs[0:n_grp], outs[n_grp:2 * n_grp], outs[2 * n_grp:3 * n_grp], outs[3 * n_grp], outs[3 * n_grp + 1]


def _proj(kernel, u, weights, out_dtypes, tm, name):
    t, d = u.shape
    row = lambda i: (i, 0)
    widths = [w.shape[1] for w in weights]
    vmem = (sum(_nbytes(w.shape, BF16) for w in weights) + 2 * _nbytes((tm, d), BF16)
            + sum(2 * _nbytes((tm, n), dt) for n, dt in zip(widths, out_dtypes))
            + 4 * _nbytes((tm, max(widths)), F32))
    return pl.pallas_call(
        kernel,
        grid=(t // tm,),
        in_specs=[pl.BlockSpec((tm, d), row)] + [_resident(w.shape) for w in weights],
        out_specs=[pl.BlockSpec((tm, n), row) for n in widths],
        out_shape=[jax.ShapeDtypeStruct((t, n), dt) for n, dt in zip(widths, out_dtypes)],
        compiler_params=_cparams(("parallel",), vmem),
        name=name,
    )(u, *weights)


SSD_CHUNK = 128


def _ssd_gate_norm(y, z, gain):
    return _rms(y * _silu(z), gain)


def _ssd_prompt_kernel(z_ref, xprev_ref, xbc_ref, dt_ref, cw_ref, cb_ref, dtb_ref, alog_ref, dskip_ref, ng_ref,
                       shift_ref, expand_ref, y_ref, st_ref, st_t):
    L = SSD_CHUNK
    c = pl.program_id(1)
    d_inner = z_ref.shape[-1]
    gw = HPG * HEAD_DIM
    gn = SSM_GROUPS * SSM_STATE

    @pl.when(c == 0)
    def _():
        st_t[...] = jnp.zeros(st_t.shape, F32)

    x_cur = xbc_ref[0]
    x_prev = jnp.where(c > 0, xprev_ref[0], jnp.zeros_like(x_cur))
    both = jnp.concatenate([x_prev, x_cur], axis=0)
    conv = cb_ref[...] + x_cur.astype(F32) * cw_ref[CONV_W - 1:CONV_W, :]
    for back in range(1, CONV_W):
        tap = CONV_W - 1 - back
        conv = conv + _dot(shift_ref[back - 1], both) * cw_ref[tap:tap + 1, :]
    xa = _silu(conv)
    xs = xa[:, :d_inner]
    bm = xa[:, d_inner:d_inner + gn]
    cm = xa[:, d_inner + gn:]

    dt = _softplus(dt_ref[0] + dtb_ref[...])
    a = -jnp.exp(alog_ref[...])
    row = lax.broadcasted_iota(jnp.int32, (L, L), 0)
    col = lax.broadcasted_iota(jnp.int32, (L, L), 1)
    causal = row >= col
    acs = _dot_sel_l(causal.astype(BF16), dt * a)
    acs_t = acs.T
    dt_t = dt.T
    acs_last = acs[L - 1:L, :]

    expand = expand_ref[...]
    e_acs =_dot(jnp.exp(acs).astype(BF16), expand)
    w_end = _dot((jnp.exp(acs_last - acs) * dt).astype(BF16), expand)
    dec_state = _dot_sel_r(jnp.broadcast_to(jnp.exp(acs_last), (SUBLANES, LANES)), expand)[0:1, :]

    xs_b = xs.astype(BF16)
    xw = (xs * w_end).astype(BF16)
    lane_first = lax.broadcasted_iota(jnp.int32, (L, LANES), 1) < HEAD_DIM
    y_groups = []
    for g in range(SSM_GROUPS):
        gsl = slice(g * gw, (g + 1) * gw)
        b_g = bm[:, g * SSM_STATE:(g + 1) * SSM_STATE]
        c_g = cm[:, g * SSM_STATE:(g + 1) * SSM_STATE].astype(BF16)
        cb = _dot_nt(c_g, b_g.astype(BF16))
        st_g = st_t[g]
        y_inter = _dot(c_g, st_g.astype(BF16))
        pairs = []
        for pr in range(HPG // 2):
            res = []
            for sub in range(2):
                h = g * HPG + pr * 2 + sub
                seg = acs[:, h:h + 1] - acs_t[h:h + 1, :]
                w = cb * jnp.exp(jnp.where(causal, seg, -jnp.inf)) * dt_t[h:h + 1, :]
                lo = g * gw + pr * LANES
                res.append(_dot(w.astype(BF16), xs_b[:, lo:lo + LANES]))
            pairs.append(jnp.where(lane_first, res[0], res[1]))
        y_groups.append(jnp.concatenate(pairs, axis=1) + y_inter * e_acs[:, gsl])
        st_t[g] = dec_state[:, gsl] * st_g + _dot(b_g.T.astype(BF16), xw[:, gsl])
    y = jnp.concatenate(y_groups, axis=1) + dskip_ref[...] * xs
    y_ref[0] = _ssd_gate_norm(y, z_ref[0].astype(F32), ng_ref[...]).astype(y_ref.dtype)

    @pl.when(c == pl.num_programs(1) - 1)
    def _():
        for g in range(SSM_GROUPS):
            st_ref[0, g * HPG:(g + 1) * HPG] = st_t[g].T.reshape(HPG, HEAD_DIM, SSM_STATE)


def _ssd_prompt(z, xbc, dt, cw, cb, dtb, alog, dskip, ng):
    b, s, d_inner = z.shape
    conv_dim = xbc.shape[-1]
    n_heads = d_inner // HEAD_DIM
    L = SSD_CHUNK
    blk = lambda n: pl.BlockSpec((1, L, n), lambda i, c: (i, c, 0))
    r = jnp.arange(L)[:, None]
    cidx = jnp.arange(2 * L)[None, :]
    shift = jnp.stack([(cidx == r + L - k) for k in range(1, CONV_W)]).astype(BF16)
    heads = jnp.arange(LANES)[:, None]
    expand = (jnp.arange(d_inner)[None, :] // HEAD_DIM == heads).astype(BF16)
    prev_blk = pl.BlockSpec((1, L, conv_dim), lambda i, c: (i, jnp.maximum(c - 1, 0), 0))
    vmem = (2 * (_nbytes((L, d_inner), z.dtype) * 2 + 2 * _nbytes((L, conv_dim), xbc.dtype) + _nbytes((L, LANES), F32))
            + 2 * _nbytes((n_heads, HEAD_DIM, SSM_STATE), F32)
            + _nbytes((SSM_GROUPS, SSM_STATE, HPG * HEAD_DIM), F32)
            + _nbytes(shift.shape, BF16) + _nbytes(expand.shape, BF16)
            + 16 * _nbytes((L, conv_dim), F32))
    return pl.pallas_call(
        _ssd_prompt_kernel,
        grid=(b, s // L),
        in_specs=[blk(d_inner), prev_blk, blk(conv_dim), blk(LANES),
                  _resident(cw.shape), _resident(cb.shape), _resident(dtb.shape), _resident(alog.shape),
                  _resident(dskip.shape), _resident(ng.shape), _resident(shift.shape), _resident(expand.shape)],
        out_specs=[blk(d_inner),
                   pl.BlockSpec((1, n_heads, HEAD_DIM, SSM_STATE), lambda i, c: (i, 0, 0, 0))],
        out_shape=[jax.ShapeDtypeStruct((b, s, d_inner), BF16),
                   jax.ShapeDtypeStruct((b, n_heads, HEAD_DIM, SSM_STATE), F32)],
        scratch_shapes=[pltpu.VMEM((SSM_GROUPS, SSM_STATE, HPG * HEAD_DIM), F32)],
        compiler_params=_cparams(("parallel", "arbitrary"), vmem),
        name="ssd_prompt_scan",
    )(z, xbc, xbc, dt, cw, cb, dtb, alog, dskip, ng, shift, expand)


MERGE_FAN = 4


def _attn_prompt_kernel(*refs, first, last):
    q_ref, kp_ref, kc_ref, vp_ref, vc_ref = refs[:5]
    refs = refs[5:]
    if not first:
        oin_refs, lin_refs = refs[:MERGE_FAN], refs[MERGE_FAN:2 * MERGE_FAN]
        refs = refs[2 * MERGE_FAN:]
    out_refs = refs
    o_ref = out_refs[0]
    n = pl.program_id(2)
    q = q_ref[...] * ATTN_SCALE
    kk = jnp.concatenate([kp_ref[...], kc_ref[...]], axis=0)
    vv = jnp.concatenate([vp_ref[...], vc_ref[...]], axis=0)
    qi = lax.broadcasted_iota(jnp.int32, (NK, 2 * NK), 0)
    kj = lax.broadcasted_iota(jnp.int32, (NK, 2 * NK), 1)
    dist = qi + NK - kj
    mask = (dist >= 0) & (dist <= NK) & ((kj >= NK) | (n > 0))
    lane = lax.broadcasted_iota(jnp.int32, (NK, LANES), 1)
    lane_first = lane < HEAD_DIM
    lse_out = jnp.zeros((NK, LANES), F32)
    o_parts = []
    for pr in range(HPG // 2):
        sl = slice(pr * LANES, (pr + 1) * LANES)
        q_pair, k_pair, v_pair = q[:, sl], kk[:, sl], vv[:, sl]
        res = []
        for sub in range(2):
            h = pr * 2 + sub
            qm = jnp.where(lane_first if sub == 0 else ~lane_first, q_pair, jnp.zeros_like(q_pair))
            s = jnp.where(mask, _dot_nt(qm, k_pair), -jnp.inf)
            m = jnp.max(s, axis=-1, keepdims=True)
            p = jnp.exp(s - m)
            l = jnp.sum(p, axis=-1, keepdims=True)
            res.append(_dot(p.astype(BF16), v_pair) / l)
            lse_out = jnp.where(lane == h, m + jnp.log(l), lse_out)
        o_parts.append(jnp.where(lane_first, res[0], res[1]))
    o = jnp.concatenate(o_parts, axis=1)
    if not first:
        rows = NK // MERGE_FAN
        dst = lax.broadcasted_iota(jnp.int32, (NK, NK), 0)
        src = lax.broadcasted_iota(jnp.int32, (NK, NK), 1)
        perm = (src == (dst % MERGE_FAN) * rows + dst // MERGE_FAN).astype(BF16)
        o_prev = _dot(perm, jnp.concatenate([r[...] for r in oin_refs], axis=0))
        lse_prev = _dot_sel_l(perm, jnp.concatenate([r[...] for r in lin_refs], axis=0))
        lse_new = jnp.maximum(lse_out, lse_prev) + jnp.log1p(jnp.exp(-jnp.abs(lse_out - lse_prev)))
        expand = _head_expand(LANES, HPG)
        w_prev = _dot_sel_r(jnp.exp(lse_prev - lse_new), expand)
        w_cur = _dot_sel_r(jnp.exp(lse_out - lse_new), expand)
        o = o_prev * w_prev + o * w_cur
        lse_out = lse_new
    o_ref[...] = o.astype(o_ref.dtype)
    if not last:
        out_refs[1][...] = lse_out


def _attn_prompt_group(q, k, v, batch, gi, first, last, o_in, lse_in):
    window, dil = DIL_GROUPS[gi]
    td = q.shape[0]
    sd = td // batch
    assert window // dil == NK and sd % NK == 0 and q.shape[1] == dil * GROUP_WIDTH
    nb = sd // NK
    blk = lambda width: pl.BlockSpec((NK, width), lambda i, r, n: (i * nb + n, r))
    prev = pl.BlockSpec((NK, GROUP_WIDTH), lambda i, r, n: (i * nb + jnp.maximum(n - 1, 0), r))
    args, in_specs = [q, k, k, v, v], [blk(GROUP_WIDTH), prev, blk(GROUP_WIDTH), prev, blk(GROUP_WIDTH)]
    if not first:
        assert o_in.shape == (td // MERGE_FAN, MERGE_FAN * dil * GROUP_WIDTH)
        piece = lambda width, c: pl.BlockSpec((NK // MERGE_FAN, width), lambda i, r, n: (i * nb + n, dil * c + r))
        args += [o_in] * MERGE_FAN + [lse_in] * MERGE_FAN
        in_specs += [piece(GROUP_WIDTH, c) for c in range(MERGE_FAN)] + [piece(LANES, c) for c in range(MERGE_FAN)]
    out_specs = [blk(GROUP_WIDTH)]
    out_shape = [jax.ShapeDtypeStruct((td, dil * GROUP_WIDTH), BF16)]
    if not last:
        out_specs.append(blk(LANES))
        out_shape.append(jax.ShapeDtypeStruct((td, dil * LANES), F32))
    vmem = (2 * 7 * _nbytes((NK, GROUP_WIDTH), BF16) + 6 * _nbytes((NK, LANES), F32)
            + 24 * _nbytes((NK, 2 * NK), F32) + 6 * _nbytes((NK, GROUP_WIDTH), F32))
    outs = pl.pallas_call(
        functools.partial(_attn_prompt_kernel, first=first, last=last),
        grid=(batch, dil, nb),
        in_specs=in_specs, out_specs=out_specs, out_shape=out_shape,
        compiler_params=_cparams(("parallel", "parallel", "parallel"), vmem),
        name=f"attn_prompt_w{window}",
    )(*args)
    return outs[0], (None if last else outs[1])


def _out_kernel(h_ref, y_ref, at_ref, gs_ref, ga_ref, p_ref,
                wos_ref, woa_ref, wo_ref, n2_ref, wg_ref, wu_ref, wd_ref, np_ref, wpg_ref, wpp_ref, nf_ref,
                o_ref):
    merged = (gs_ref[...].astype(F32) * _dot(y_ref[...].astype(BF16), wos_ref[...])
              + ga_ref[...].astype(F32) * _dot(at_ref[...].astype(BF16), woa_ref[...]))
    h = h_ref[...] + _dot(merged.astype(BF16), wo_ref[...])
    h = _swiglu_half(h, n2_ref[...], wg_ref, wu_ref, wd_ref)
    gate = _sigmoid(_dot(_rms(h, np_ref[...]).astype(BF16), wpg_ref[...]))
    h = h + gate * _dot(p_ref[...].astype(BF16), wpp_ref[...])
    o_ref[...] = _rms(h, nf_ref[...])


def _out_stage(h, y, attn, gs, ga, p, weights, tm):
    t, d = h.shape
    row = lambda i: (i, 0)
    acts = [h, y, attn, gs, ga, p]
    vmem = (sum(_nbytes(w.shape, w.dtype) for w in weights)
            + sum(2 * _nbytes((tm, a.shape[1]), a.dtype) for a in acts) + 2 * _nbytes((tm, d), F32)
            + 8 * _nbytes((tm, d), F32) + 4 * _nbytes((tm, FF_CHUNK), F32))
    return pl.pallas_call(
        _out_kernel,
        grid=(t // tm,),
        in_specs=[pl.BlockSpec((tm, a.shape[1]), row) for a in acts] + [_resident(w.shape) for w in weights],
        out_specs=pl.BlockSpec((tm, d), row),
        out_shape=jax.ShapeDtypeStruct((t, d), F32),
        compiler_params=_cparams(("parallel",), vmem),
        name="output_stage",
    )(*acts, *weights)


def _ssd_sample_prep_kernel(xn_ref, sc_ref, dtr_ref, cw_ref, cb_ref, dtb_ref, alog_ref,
                            xs_ref, xdt_t_ref, bdt_ref, c_t_ref, dec_ref):
    conv_dim = xn_ref.shape[1]
    d_inner = xs_ref.shape[1]
    gn = SSM_GROUPS * SSM_STATE
    conv = cb_ref[...] + xn_ref[...] * cw_ref[CONV_W - 1:CONV_W, :]
    for tap in range(CONV_W - 1):
        conv = conv + sc_ref[:, tap * conv_dim:(tap + 1) * conv_dim] * cw_ref[tap:tap + 1, :]
    xa = _silu(conv)
    xs = xa[:, :d_inner]
    xs_ref[...] = xs
    dt = _softplus(dtr_ref[...] + dtb_ref[...])
    dec_ref[...] = jnp.exp(dt * -jnp.exp(alog_ref[...]))
    n_heads = d_inner // HEAD_DIM
    dt_x = _dot_sel_r(dt, _head_expand(LANES, n_heads))
    xdt_t_ref[...] = (xs * dt_x).T.astype(xdt_t_ref.dtype)
    bdt_ref[...] = xa[:, d_inner:d_inner + gn]
    for g in range(SSM_GROUPS):
        lo = d_inner + gn + g * SSM_STATE
        c_t_ref[g] = xa[:, lo:lo + SSM_STATE].T.astype(c_t_ref.dtype)


def _ssd_sample_prep(xn, sc, dtr, cw, cb, dtb, alog, d_inner):
    nb, conv_dim = xn.shape
    args = [xn, sc, dtr, cw, cb, dtb, alog]
    out_shape = [jax.ShapeDtypeStruct((nb, d_inner), F32),
                 jax.ShapeDtypeStruct((d_inner, nb), BF16),
                 jax.ShapeDtypeStruct((nb, SSM_GROUPS * SSM_STATE), F32),
                 jax.ShapeDtypeStruct((SSM_GROUPS, SSM_STATE, nb), BF16),
                 jax.ShapeDtypeStruct((nb, LANES), F32)]
    vmem = 2 * sum(_nbytes(a.shape, a.dtype) for a in args) + 2 * sum(_nbytes(o.shape, o.dtype) for o in out_shape) \
        + 8 * _nbytes((nb, conv_dim), F32)
    return pl.pallas_call(
        _ssd_sample_prep_kernel,
        grid=(1,),
        in_specs=[_resident(a.shape) for a in args],
        out_specs=[pl.BlockSpec(o.shape, functools.partial(lambda nd, i: (0,) * nd, len(o.shape))) for o in out_shape],
        out_shape=out_shape,
        compiler_params=_cparams(("arbitrary",), vmem),
        name="ssd_sample_prep",
    )(*args)


SAMPLE_STATE_TILE = 4


def _ssd_sample_state_kernel(dec_ref, st_ref, xdt_t_ref, bdt_ref, c_t_ref, sto_ref, y_t_ref):
    i = pl.program_id(0)
    nb = bdt_ref.shape[0]
    n_heads = st_ref.shape[1]
    gw = HPG * HEAD_DIM

    @pl.when(i == 0)
    def _():
        y_t_ref[...] = jnp.zeros(y_t_ref.shape, F32)

    row = lax.broadcasted_iota(jnp.int32, (nb, SSM_STATE), 0)
    lane = lax.broadcasted_iota(jnp.int32, (gw, nb), 1)
    for j in range(SAMPLE_STATE_TILE):
        b = i * SAMPLE_STATE_TILE + j
        for g in range(SSM_GROUPS):
            b_row = jnp.where(row == b, bdt_ref[:, g * SSM_STATE:(g + 1) * SSM_STATE], 0.0).astype(BF16)
            upd = _dot(xdt_t_ref[g * gw:(g + 1) * gw, :], b_row)
            new = []
            for hg in range(HPG):
                h = g * HPG + hg
                new_h = st_ref[j, h] * dec_ref[b * n_heads + h] + upd[hg * HEAD_DIM:(hg + 1) * HEAD_DIM, :]
                sto_ref[j, h] = new_h
                new.append(new_h)
            yg = _dot(jnp.concatenate(new, axis=0).astype(BF16), c_t_ref[g])
            y_t_ref[g * gw:(g + 1) * gw, :] += jnp.where(lane == b, yg, 0.0)


def _ssd_sample_state(dec, st, xdt_t, bdt, c_t):
    nb, n_heads = st.shape[0], st.shape[1]
    d_inner = n_heads * HEAD_DIM
    bt = SAMPLE_STATE_TILE
    st_spec = pl.BlockSpec((bt, n_heads, HEAD_DIM, SSM_STATE), lambda i: (i, 0, 0, 0))
    vmem = (4 * _nbytes((bt, n_heads, HEAD_DIM, SSM_STATE), F32) + _nbytes(xdt_t.shape, BF16)
            + _nbytes(bdt.shape, F32) + _nbytes(c_t.shape, BF16) + 2 * _nbytes((d_inner, nb), F32)
            + 16 * _nbytes((HPG * HEAD_DIM, SSM_STATE), F32))
    return pl.pallas_call(
        _ssd_sample_state_kernel,
        grid=(nb // bt,),
        in_specs=[pl.BlockSpec(memory_space=pltpu.SMEM), st_spec,
                  _resident(xdt_t.shape), _resident(bdt.shape), _resident(c_t.shape)],
        out_specs=[st_spec, pl.BlockSpec((d_inner, nb), lambda i: (0, 0))],
        out_shape=[jax.ShapeDtypeStruct(st.shape, F32), jax.ShapeDtypeStruct((d_inner, nb), F32)],
        compiler_params=_cparams(("arbitrary",), vmem),
        name="ssd_sample_state",
    )(dec, st, xdt_t, bdt, c_t)


def _ssd_sample_post_kernel(y_t_ref, xs_ref, z_ref, dskip_ref, ng_ref, y_ref):
    y = y_t_ref[...].T + dskip_ref[...] * xs_ref[...]
    y_ref[...] = _ssd_gate_norm(y, z_ref[...], ng_ref[...]).astype(y_ref.dtype)


def _ssd_sample_post(y_t, xs, z, dskip, ng):
    args = [y_t, xs, z, dskip, ng]
    vmem = 2 * sum(_nbytes(a.shape, a.dtype) for a in args) + 8 * _nbytes(xs.shape, F32)
    return pl.pallas_call(
        _ssd_sample_post_kernel,
        grid=(1,),
        in_specs=[_resident(a.shape) for a in args],
        out_specs=pl.BlockSpec(xs.shape, lambda i: (0, 0)),
        out_shape=jax.ShapeDtypeStruct(xs.shape, F32),
        compiler_params=_cparams(("arbitrary",), vmem),
        name="ssd_sample_post",
    )(*args)


def _attn_sample_kernel(q_ref, k_ref, v_ref, c0_ref, c1_ref, c2_ref, o_ref):
    scores, values = [], []
    for gi, c_ref in enumerate((c0_ref, c1_ref, c2_ref)):
        hs = slice(gi * HPG, (gi + 1) * HPG)
        qg = q_ref[0, hs, :] * ATTN_SCALE
        s_cache = jnp.sum(c_ref[0, :, 0, 0] * qg[None], axis=-1, keepdims=True)
        s_new = jnp.sum(k_ref[0, hs, :] * qg, axis=-1, keepdims=True)
        scores.append((s_cache, s_new))
        values.append((c_ref[0, :, 0, 1], v_ref[0, hs, :]))
    m = functools.reduce(jnp.maximum, [jnp.maximum(jnp.max(sc, axis=0), sn) for sc, sn in scores])
    num = jnp.zeros((HPG, HEAD_DIM), F32)
    den = jnp.zeros((HPG, 1), F32)
    for (sc, sn), (vc, vn) in zip(scores, values):
        p_c = jnp.exp(sc - m[None])
        p_n = jnp.exp(sn - m)
        den = den + jnp.sum(p_c, axis=0) + p_n
        num = num + jnp.sum(p_c * vc, axis=0) + p_n * vn
    o_ref[0] = num / den


def _attn_sample(q, k, v, caches):
    nb, n_heads, _ = q.shape
    views, specs = [], []
    for (window, dil), c in zip(DIL_GROUPS, caches):
        assert c.shape[1] == window and window // dil == NK
        views.append(c.reshape(nb, NK, dil, 2, HPG, HEAD_DIM))
        specs.append(pl.BlockSpec((1, NK, 1, 2, HPG, HEAD_DIM), lambda i: (i, 0, 0, 0, 0, 0)))
    vec = pl.BlockSpec((1, n_heads, HEAD_DIM), lambda i: (i, 0, 0))
    padded_tile = _nbytes((SUBLANES, LANES), F32)
    vmem = 2 * 3 * 2 * NK * padded_tile + 12 * NK * padded_tile
    return pl.pallas_call(
        _attn_sample_kernel,
        grid=(nb,),
        in_specs=[vec, vec, vec] + specs,
        out_specs=pl.BlockSpec((1, HPG, HEAD_DIM), lambda i: (i, 0, 0)),
        out_shape=jax.ShapeDtypeStruct((nb, HPG, HEAD_DIM), F32),
        compiler_params=_cparams(("parallel",), vmem),
        name="attn_sample",
    )(q, k, v, *views)


def _row(v, width=None):
    v = v.astype(F32).reshape(1, -1)
    if width is not None and v.shape[1] < width:
        v = jnp.pad(v, ((0, 0), (0, width - v.shape[1])))
    return v


def _layer_weights(prm):
    d_model = prm['w_in'].shape[0]
    d_inner = prm['norm_ssm'].shape[0]
    n_heads = d_inner // HEAD_DIM
    conv_dim = prm['conv_w'].shape[1]
    attn_w = len(DIL_GROUPS) * HPG * HEAD_DIM
    splits = [d_inner, conv_dim, n_heads, attn_w, attn_w, attn_w, d_model, d_model]
    offs = [0]
    for n in splits:
        offs.append(offs[-1] + n)
    w_in = prm['w_in']
    seg = [w_in[:, offs[j]:offs[j + 1]].astype(BF16) for j in range(len(splits))]
    seg[2] = jnp.pad(seg[2], ((0, 0), (0, LANES - n_heads)))
    bf = lambda name: prm[name].astype(BF16)
    return dict(
        ffn1=(_row(prm['norm_ffn1']), bf('w_ffn1_gate'), bf('w_ffn1_up'), bf('w_ffn1_down'), _row(prm['norm_mix'])),
        proj_ssm=seg[0:3], proj_attn=seg[3:8],
        conv_w=prm['conv_w'].astype(F32), conv_b=_row(prm['conv_b']),
        dt_bias=_row(prm['dt_bias'], LANES), a_log=_row(prm['a_log'], LANES),
        d_skip=_row(jnp.repeat(prm['d_skip'], HEAD_DIM)), norm_ssm=_row(prm['norm_ssm']),
        out=(bf('w_o_ssm'), bf('w_o_attn'), bf('w_out'), _row(prm['norm_ffn2']), bf('w_ffn2_gate'),
             bf('w_ffn2_up'), bf('w_ffn2_down'), _row(prm['norm_ple']), bf('w_ple_gate'), bf('w_ple_proj')),
        d_inner=d_inner, conv_dim=conv_dim,
    )


def _kv_stack(k, v, gi, keep):
    b, s, _ = k.shape
    gw = HPG * HEAD_DIM
    sel = lambda t: t[:, s - keep:, gi * gw:(gi + 1) * gw].astype(F32).reshape(b, keep, HPG, HEAD_DIM)
    return jnp.stack([sel(k), sel(v)], axis=2)


def _kv_stack_wide(k, v, batch, dil, keep):
    rows = k.shape[0] // batch
    sel = lambda t: (t.reshape(batch, rows, dil * GROUP_WIDTH)[:, rows - keep // dil:, :]
                     .astype(F32).reshape(batch, keep, HPG, HEAD_DIM))
    return jnp.stack([sel(k), sel(v)], axis=2)


def _prompt_layer(x, p_emb, w, norm_final, tm):
    b, s, d = x.shape
    t = b * s
    h1, u = _ffn(x.reshape(t, d), *w['ffn1'], tm)
    z, xbc, dt = _proj(_proj_ssm_kernel, u, w['proj_ssm'], (BF16, BF16, F32), tm, "proj_ssm")
    qs, ks, vs, gs, ga = _proj_attn_wide(u, w['proj_attn'], s, tm)
    r3 = lambda a: a.reshape(b, s, a.shape[-1])
    xbc3 = r3(xbc)
    y_ssm, ssm_new = _ssd_prompt(r3(z), xbc3, r3(dt), w['conv_w'], w['conv_b'], w['dt_bias'], w['a_log'],
                                 w['d_skip'], w['norm_ssm'])
    order = sorted(range(len(DIL_GROUPS)), key=lambda gi: -DIL_GROUPS[gi][1])
    assert all(DIL_GROUPS[a][1] == MERGE_FAN * DIL_GROUPS[c][1] for a, c in zip(order, order[1:]))
    assert DIL_GROUPS[order[-1]][1] == 1
    o, lse = None, None
    for j, gi in enumerate(order):
        o, lse = _attn_prompt_group(qs[gi], ks[gi], vs[gi], b, gi, j == 0, j == len(order) - 1, o, lse)
    y = _out_stage(h1, y_ssm.reshape(t, -1), o, gs, ga, p_emb.reshape(t, -1),
                   w['out'] + (_row(norm_final),), OUT_ROW_TILE)
    kv = [_kv_stack_wide(ks[gi], vs[gi], b, dil, min(window, s)) for gi, (window, dil) in enumerate(DIL_GROUPS)]
    conv_new = xbc3[:, s - (CONV_W - 1):, :].astype(F32)
    return y.reshape(b, s, d), kv, conv_new, ssm_new


def _sample_layer(x, p_emb, w, norm_final, conv_prev, ssm_prev, caches):
    nb, s, d = x.shape
    assert s == 1
    h1, u = _ffn(x.reshape(nb, d), *w['ffn1'], nb)
    z, xbc, dt = _proj(_proj_ssm_kernel, u, w['proj_ssm'], (F32, F32, F32), nb, "proj_ssm_sample")
    q, k, v, gs, ga = _proj(functools.partial(_proj_attn_kernel, seq=1, pos0=PAST_LEN), u, w['proj_attn'],
                            (F32,) * 5, nb, "proj_attn_sample")
    xs, xdt_t, bdt, c_t, dec = _ssd_sample_prep(xbc, conv_prev.reshape(nb, -1), dt, w['conv_w'], w['conv_b'],
                                                w['dt_bias'], w['a_log'], w['d_inner'])
    dec_flat = dec[:, :ssm_prev.shape[1]].reshape(-1)
    ssm_new, y_t = _ssd_sample_state(dec_flat, ssm_prev, xdt_t, bdt, c_t)
    y_ssm = _ssd_sample_post(y_t, xs, z, w['d_skip'], w['norm_ssm'])
    by_head = lambda a: a.reshape(nb, -1, HEAD_DIM)
    attn = _attn_sample(by_head(q), by_head(k), by_head(v), caches)
    y = _out_stage(h1, y_ssm, attn.reshape(nb, -1), gs, ga, p_emb.reshape(nb, -1),
                   w['out'] + (_row(norm_final),), nb)
    r3 = lambda a: a.reshape(nb, 1, a.shape[-1])
    kv = [_kv_stack(r3(k), r3(v), gi, 1) for gi in range(len(DIL_GROUPS))]
    conv_new = jnp.concatenate([conv_prev[:, 1:], xbc[:, None, :]], axis=1)
    return y.reshape(nb, 1, d), kv, conv_new, ssm_new


PROMPT_ROW_TILE = 512
OUT_ROW_TILE = 256


def kernel(x_prompt, x_sample, cache_kv_w128, cache_kv_w512, cache_kv_w2048, state_conv, state_ssm, p_prompt, p_sample, norm_ffn1, w_ffn1_gate, w_ffn1_up, w_ffn1_down, norm_mix, w_in, conv_w, conv_b, dt_bias, a_log, d_skip, norm_ssm, w_o_ssm, w_o_attn, w_out, norm_ffn2, w_ffn2_gate, w_ffn2_up, w_ffn2_down, norm_ple, w_ple_gate, w_ple_proj, norm_final):
    depth = w_in.shape[0]
    assert depth == 1
    layer_params = dict(
        norm_ffn1=norm_ffn1, w_ffn1_gate=w_ffn1_gate, w_ffn1_up=w_ffn1_up, w_ffn1_down=w_ffn1_down,
        norm_mix=norm_mix, w_in=w_in, conv_w=conv_w, conv_b=conv_b, dt_bias=dt_bias, a_log=a_log,
        d_skip=d_skip, norm_ssm=norm_ssm, w_o_ssm=w_o_ssm, w_o_attn=w_o_attn, w_out=w_out,
        norm_ffn2=norm_ffn2, w_ffn2_gate=w_ffn2_gate, w_ffn2_up=w_ffn2_up, w_ffn2_down=w_ffn2_down,
        norm_ple=norm_ple, w_ple_gate=w_ple_gate, w_ple_proj=w_ple_proj)
    i = 0
    w = _layer_weights({name: val[i] for name, val in layer_params.items()})
    yp, kvp, convp, ssmp = _prompt_layer(x_prompt, p_prompt[i], w, norm_final, PROMPT_ROW_TILE)
    ys, kvs, convs, ssms = _sample_layer(x_sample, p_sample[i], w, norm_final, state_conv[i], state_ssm[i],
                                         (cache_kv_w128[i], cache_kv_w512[i], cache_kv_w2048[i]))
    st = lambda a: a[None]
    return (yp, ys, st(kvp[0]), st(kvp[1]), st(kvp[2]), st(convp), st(ssmp),
            st(kvs[0]), st(kvs[1]), st(kvs[2]), st(convs), st(ssms))
```

```python
import functools
import math

import jax
import jax.numpy as jnp
from jax import lax
from jax.experimental import pallas as pl
from jax.experimental.pallas import tpu as pltpu

F32 = jnp.float32
BF16 = jnp.bfloat16

EPS = 1e-6
ROPE_THETA = 10000.0
PAST_LEN = 8192
HEAD_DIM = 64
HPG = 8
SSM_GROUPS = 4
SSM_STATE = 128
CONV_W = 4
DIL_GROUPS = ((128, 1), (512, 4), (2048, 16))
ATTN_SCALE = HEAD_DIM ** -0.5
NK = 128

LANES = 128
SUBLANES = 8
VMEM_LIMIT_CAP = 56 * 1024 * 1024


def _cparams(sem, vmem_bytes):
    return pltpu.CompilerParams(dimension_semantics=sem,
                                vmem_limit_bytes=int(min(VMEM_LIMIT_CAP, vmem_bytes)))


def _resident(shape):
    nd = len(shape)
    return pl.BlockSpec(shape, lambda *_: (0,) * nd, pipeline_mode=pl.Buffered(1))


def _nbytes(shape, dtype):
    return math.prod(shape) * jnp.dtype(dtype).itemsize


def _rms(x, gain):
    ms = jnp.mean(x * x, axis=-1, keepdims=True)
    return x * lax.rsqrt(ms + EPS) * gain


def _sigmoid(x):
    return 1.0 / (1.0 + jnp.exp(-x))


def _silu(x):
    return x * _sigmoid(x)


def _softplus(x):
    return jnp.maximum(x, 0.0) + jnp.log1p(jnp.exp(-jnp.abs(x)))


def _dot(a, b):
    return jnp.dot(a, b, preferred_element_type=F32)


def _dot_nt(a, b):
    return lax.dot_general(a, b, (((1,), (1,)), ((), ())), preferred_element_type=F32)


def _split3(x):
    hi = x.astype(BF16)
    r1 = x - hi.astype(F32)
    mid = r1.astype(BF16)
    lo = (r1 - mid.astype(F32)).astype(BF16)
    return hi, mid, lo


def _dot_sel_l(sel, x):
    hi, mid, lo = _split3(x)
    return _dot(sel, hi) + _dot(sel, mid) + _dot(sel, lo)


def _dot_sel_r(x, sel):
    hi, mid, lo = _split3(x)
    return _dot(hi, sel) + _dot(mid, sel) + _dot(lo, sel)


def _head_expand(n_heads_padded, n_heads):
    r = lax.broadcasted_iota(jnp.int32, (n_heads_padded, n_heads * HEAD_DIM), 0)
    c = lax.broadcasted_iota(jnp.int32, (n_heads_padded, n_heads * HEAD_DIM), 1)
    return (c // HEAD_DIM == r).astype(BF16)


FF_CHUNK = 256


def _swiglu_half(x, gain, wg_ref, wu_ref, wd_ref):
    u = _rms(x, gain).astype(BF16)
    d_ff = wg_ref.shape[1]
    acc = jnp.zeros(x.shape, F32)
    for c in range(d_ff // FF_CHUNK):
        sl = slice(c * FF_CHUNK, (c + 1) * FF_CHUNK)
        g = _dot(u, wg_ref[:, sl])
        up = _dot(u, wu_ref[:, sl])
        acc = acc + _dot((_silu(g) * up).astype(BF16), wd_ref[sl, :])
    return x + 0.5 * acc


def _ffn_kernel(x_ref, n1_ref, wg_ref, wu_ref, wd_ref, n2_ref, h_ref, u_ref):
    h = _swiglu_half(x_ref[...], n1_ref[...], wg_ref, wu_ref, wd_ref)
    h_ref[...] = h
    u_ref[...] = _rms(h, n2_ref[...]).astype(u_ref.dtype)


def _ffn(x, n1, wg, wu, wd, n2, tm):
    t, d = x.shape
    d_ff = wg.shape[1]
    vmem = (3 * _nbytes((d, d_ff), BF16) + 2 * 2 * _nbytes((tm, d), F32) + 2 * _nbytes((tm, d), BF16)
            + 6 * _nbytes((tm, d), F32) + 4 * _nbytes((tm, FF_CHUNK), F32))
    return pl.pallas_call(
        _ffn_kernel,
        grid=(t // tm,),
        in_specs=[pl.BlockSpec((tm, d), lambda i: (i, 0)),
                  _resident((1, d)), _resident((d, d_ff)), _resident((d, d_ff)), _resident((d_ff, d)),
                  _resident((1, d))],
        out_specs=[pl.BlockSpec((tm, d), lambda i: (i, 0)), pl.BlockSpec((tm, d), lambda i: (i, 0))],
        out_shape=[jax.ShapeDtypeStruct((t, d), F32), jax.ShapeDtypeStruct((t, d), BF16)],
        compiler_params=_cparams(("parallel",), vmem),
        name="ffn_half_step",
    )(x, n1, wg, wu, wd, n2)


def _rope_tables(tm, seq, pos0):
    row = lax.broadcasted_iota(jnp.int32, (tm, LANES), 0) + pl.program_id(0) * tm
    pos = (row % seq + pos0).astype(F32)
    lane = lax.broadcasted_iota(jnp.int32, (tm, LANES), 1)
    half = HEAD_DIM // 2
    j = (lane % half).astype(F32)
    inv_freq = jnp.exp(j * (-math.log(ROPE_THETA) / half))
    ang = pos * inv_freq
    first = (lane % HEAD_DIM) < half
    return jnp.cos(ang), jnp.where(first, -jnp.sin(ang), jnp.sin(ang)), first


def _rotary(x, cos, sin_signed, first):
    outs = []
    for c in range(x.shape[1] // LANES):
        xb = x[:, c * LANES:(c + 1) * LANES]
        other = jnp.where(first, pltpu.roll(xb, LANES - HEAD_DIM // 2, 1), pltpu.roll(xb, HEAD_DIM // 2, 1))
        outs.append(xb * cos + other * sin_signed)
    return jnp.concatenate(outs, axis=1)


def _proj_ssm_kernel(u_ref, wz_ref, wx_ref, wdt_ref, z_ref, xbc_ref, dt_ref):
    u = u_ref[...]
    z_ref[...] = _dot(u, wz_ref[...]).astype(z_ref.dtype)
    xbc_ref[...] = _dot(u, wx_ref[...]).astype(xbc_ref.dtype)
    dt_ref[...] = _dot(u, wdt_ref[...])


def _proj_attn_kernel(u_ref, wq_ref, wk_ref, wv_ref, wgs_ref, wga_ref,
                      q_ref, k_ref, v_ref, gs_ref, ga_ref, *, seq, pos0):
    u = u_ref[...]
    cos, sin_signed, first = _rope_tables(u.shape[0], seq, pos0)
    q_ref[...] = _rotary(_dot(u, wq_ref[...]), cos, sin_signed, first).astype(q_ref.dtype)
    k_ref[...] = _rotary(_dot(u, wk_ref[...]), cos, sin_signed, first).astype(k_ref.dtype)
    v_ref[...] = _dot(u, wv_ref[...]).astype(v_ref.dtype)
    gs_ref[...] = _sigmoid(_dot(u, wgs_ref[...])).astype(gs_ref.dtype)
    ga_ref[...] = _sigmoid(_dot(u, wga_ref[...])).astype(ga_ref.dtype)


GROUP_WIDTH = HPG * HEAD_DIM


def _store_wide(stage, val, out_ref, dil):
    if dil == 1:
        out_ref[...] = val.astype(out_ref.dtype)
        return
    rows = val.shape[0] // dil
    for j in range(GROUP_WIDTH // LANES):
        stage[j] = val[:, j * LANES:(j + 1) * LANES]
    for r in range(dil):
        for j in range(GROUP_WIDTH // LANES):
            lo = r * GROUP_WIDTH + j * LANES
            out_ref[:, lo:lo + LANES] = stage[j, pl.ds(r, rows, stride=dil), :].astype(out_ref.dtype)


def _proj_attn_wide_kernel(u_ref, wq_ref, wk_ref, wv_ref, wgs_ref, wga_ref, *refs, seq):
    n_grp = len(DIL_GROUPS)
    q_refs, k_refs, v_refs = refs[0:n_grp], refs[n_grp:2 * n_grp], refs[2 * n_grp:3 * n_grp]
    gs_ref, ga_ref, stage = refs[3 * n_grp:]
    u = u_ref[...]
    cos, sin_signed, first = _rope_tables(u.shape[0], seq, 0)
    for w_ref, out_refs, rope in ((wq_ref, q_refs, True), (wk_ref, k_refs, True), (wv_ref, v_refs, False)):
        res = _dot(u, w_ref[...])
        if rope:
            res = _rotary(res, cos, sin_signed, first)
        for gi, (_, dil) in enumerate(DIL_GROUPS):
            _store_wide(stage, res[:, gi * GROUP_WIDTH:(gi + 1) * GROUP_WIDTH], out_refs[gi], dil)
    gs_ref[...] = _sigmoid(_dot(u, wgs_ref[...])).astype(gs_ref.dtype)
    ga_ref[...] = _sigmoid(_dot(u, wga_ref[...])).astype(ga_ref.dtype)


def _proj_attn_wide(u, weights, seq, tm):
    t, d = u.shape
    d_model = weights[3].shape[1]
    row = lambda i: (i, 0)
    qkv_specs = [pl.BlockSpec((tm // dil, dil * GROUP_WIDTH), row) for _, dil in DIL_GROUPS] * 3
    qkv_shapes = [jax.ShapeDtypeStruct((t // dil, dil * GROUP_WIDTH), BF16) for _, dil in DIL_GROUPS] * 3
    vmem = (sum(_nbytes(w.shape, BF16) for w in weights) + 2 * _nbytes((tm, d), BF16)
            + 2 * _nbytes((tm, 3 * len(DIL_GROUPS) * GROUP_WIDTH + 2 * d_model), BF16)
            + _nbytes((tm, GROUP_WIDTH), F32) + 6 * _nbytes((tm, weights[0].shape[1]), F32))
    outs = pl.pallas_call(
        functools.partial(_proj_attn_wide_kernel, seq=seq),
        grid=(t // tm,),
        in_specs=[pl.BlockSpec((tm, d), row)] + [_resident(w.shape) for w in weights],
        out_specs=qkv_specs + [pl.BlockSpec((tm, d_model), row)] * 2,
        out_shape=qkv_shapes + [jax.ShapeDtypeStruct((t, d_model), BF16)] * 2,
        scratch_shapes=[pltpu.VMEM((GROUP_WIDTH // LANES, tm, LANES), F32)],
        compiler_params=_cparams(("parallel",), vmem),
        name="proj_attn",
    )(u, *weights)
    n_grp = len(DIL_GROUPS)
    return outs[0:n_grp], outs[n_grp:2 * n_grp], outs[2 * n_grp:3 * n_grp], outs[3 * n_grp], outs[3 * n_grp + 1]


def _proj(kernel, u, weights, out_dtypes, tm, name):
    t, d = u.shape
    row = lambda i: (i, 0)
    widths = [w.shape[1] for w in weights]
    vmem = (sum(_nbytes(w.shape, BF16) for w in weights) + 2 * _nbytes((tm, d), BF16)
            + sum(2 * _nbytes((tm, n), dt) for n, dt in zip(widths, out_dtypes))
            + 4 * _nbytes((tm, max(widths)), F32))
    return pl.pallas_call(
        kernel,
        grid=(t // tm,),
        in_specs=[pl.BlockSpec((tm, d), row)] + [_resident(w.shape) for w in weights],
        out_specs=[pl.BlockSpec((tm, n), row) for n in widths],
        out_shape=[jax.ShapeDtypeStruct((t, n), dt) for n, dt in zip(widths, out_dtypes)],
        compiler_params=_cparams(("parallel",), vmem),
        name=name,
    )(u, *weights)


SSD_CHUNK = 128


def _ssd_gate_norm(y, z, gain):
    return _rms(y * _silu(z), gain)


def _ssd_prompt_kernel(z_ref, xprev_ref, xbc_ref, dt_ref, cw_ref, cb_ref, dtb_ref, alog_ref, dskip_ref, ng_ref,
                       shift_ref, expand_ref, y_ref, st_ref, st_t):
    L = SSD_CHUNK
    c = pl.program_id(1)
    d_inner = z_ref.shape[-1]
    gw = HPG * HEAD_DIM
    gn = SSM_GROUPS * SSM_STATE

    @pl.when(c == 0)
    def _():
        st_t[...] = jnp.zeros(st_t.shape, F32)

    x_cur = xbc_ref[0]
    x_prev = jnp.where(c > 0, xprev_ref[0], jnp.zeros_like(x_cur))
    both = jnp.concatenate([x_prev, x_cur], axis=0)
    conv = cb_ref[...] + x_cur.astype(F32) * cw_ref[CONV_W - 1:CONV_W, :]
    shifted = _dot(shift_ref[...], both)
    for back in range(1, CONV_W):
        tap = CONV_W - 1 - back
        conv = conv + shifted[(back - 1) * L:back * L, :] * cw_ref[tap:tap + 1, :]
    xa = _silu(conv)
    xs = xa[:, :d_inner]
    bm = xa[:, d_inner:d_inner + gn]
    cm = xa[:, d_inner + gn:]

    dt = _softplus(dt_ref[0] + dtb_ref[...])
    a = -jnp.exp(alog_ref[...])
    row = lax.broadcasted_iota(jnp.int32, (L, L), 0)
    col = lax.broadcasted_iota(jnp.int32, (L, L), 1)
    causal = row >= col
    acs = _dot_sel_l(causal.astype(BF16), dt * a)
    acs_t = acs.T
    dt_t = dt.T
    acs_last = acs[L - 1:L, :]

    expand = expand_ref[...]
    per_head = jnp.concatenate([jnp.exp(acs), jnp.exp(acs_last - acs) * dt], axis=0)
    per_lane = _dot(per_head.astype(BF16), expand)
    e_acs, w_end = per_lane[:L], per_lane[L:]
    dec_state = _dot_sel_r(jnp.broadcast_to(jnp.exp(acs_last), (SUBLANES, LANES)), expand)[0:1, :]

    xs_b = xs.astype(BF16)
    xw = (xs * w_end).astype(BF16)
    lane_first = lax.broadcasted_iota(jnp.int32, (L, LANES), 1) < HEAD_DIM
    y_groups = []
    for g in range(SSM_GROUPS):
        gsl = slice(g * gw, (g + 1) * gw)
        b_g = bm[:, g * SSM_STATE:(g + 1) * SSM_STATE]
        c_g = cm[:, g * SSM_STATE:(g + 1) * SSM_STATE].astype(BF16)
        cb = _dot_nt(c_g, b_g.astype(BF16))
        st_g = st_t[g]
        y_inter = _dot(c_g, st_g.astype(BF16))
        pairs = []
        for pr in range(HPG // 2):
            w_pair = []
            for sub in range(2):
                h = g * HPG + pr * 2 + sub
                seg = acs[:, h:h + 1] - acs_t[h:h + 1, :]
                w = cb * jnp.exp(jnp.where(causal, seg, -jnp.inf)) * dt_t[h:h + 1, :]
                w_pair.append(w.astype(BF16))
            lo = g * gw + pr * LANES
            res = _dot(jnp.concatenate(w_pair, axis=0), xs_b[:, lo:lo + LANES])
            pairs.append(jnp.where(lane_first, res[:L], res[L:]))
        y_groups.append(jnp.concatenate(pairs, axis=1) + y_inter * e_acs[:, gsl])
        st_t[g] = dec_state[:, gsl] * st_g + _dot(b_g.T.astype(BF16), xw[:, gsl])
    y = jnp.concatenate(y_groups, axis=1) + dskip_ref[...] * xs
    y_ref[0] = _ssd_gate_norm(y, z_ref[0].astype(F32), ng_ref[...]).astype(y_ref.dtype)

    @pl.when(c == pl.num_programs(1) - 1)
    def _():
        for g in range(SSM_GROUPS):
            st_ref[0, g * HPG:(g + 1) * HPG] = st_t[g].T.reshape(HPG, HEAD_DIM, SSM_STATE)


def _ssd_prompt(z, xbc, dt, cw, cb, dtb, alog, dskip, ng):
    b, s, d_inner = z.shape
    conv_dim = xbc.shape[-1]
    n_heads = d_inner // HEAD_DIM
    L = SSD_CHUNK
    blk = lambda n: pl.BlockSpec((1, L, n), lambda i, c: (i, c, 0))
    r = jnp.arange(L)[:, None]
    cidx = jnp.arange(2 * L)[None, :]
    shift = jnp.concatenate([(cidx == r + L - k) for k in range(1, CONV_W)], axis=0).astype(BF16)
    heads = jnp.arange(LANES)[:, None]
    expand = (jnp.arange(d_inner)[None, :] // HEAD_DIM == heads).astype(BF16)
    prev_blk = pl.BlockSpec((1, L, conv_dim), lambda i, c: (i, jnp.maximum(c - 1, 0), 0))
    vmem = (2 * (_nbytes((L, d_inner), z.dtype) * 2 + 2 * _nbytes((L, conv_dim), xbc.dtype) + _nbytes((L, LANES), F32))
            + 2 * _nbytes((n_heads, HEAD_DIM, SSM_STATE), F32)
            + _nbytes((SSM_GROUPS, SSM_STATE, HPG * HEAD_DIM), F32)
            + _nbytes(shift.shape, BF16) + _nbytes(expand.shape, BF16)
            + 16 * _nbytes((L, conv_dim), F32))
    return pl.pallas_call(
        _ssd_prompt_kernel,
        grid=(b, s // L),
        in_specs=[blk(d_inner), prev_blk, blk(conv_dim), blk(LANES),
                  _resident(cw.shape), _resident(cb.shape), _resident(dtb.shape), _resident(alog.shape),
                  _resident(dskip.shape), _resident(ng.shape), _resident(shift.shape), _resident(expand.shape)],
        out_specs=[blk(d_inner),
                   pl.BlockSpec((1, n_heads, HEAD_DIM, SSM_STATE), lambda i, c: (i, 0, 0, 0))],
        out_shape=[jax.ShapeDtypeStruct((b, s, d_inner), BF16),
                   jax.ShapeDtypeStruct((b, n_heads, HEAD_DIM, SSM_STATE), F32)],
        scratch_shapes=[pltpu.VMEM((SSM_GROUPS, SSM_STATE, HPG * HEAD_DIM), F32)],
        compiler_params=_cparams(("parallel", "arbitrary"), vmem),
        name="ssd_prompt_scan",
    )(z, xbc, xbc, dt, cw, cb, dtb, alog, dskip, ng, shift, expand)


MERGE_FAN = 4


def _attn_prompt_kernel(*refs, first, last):
    q_ref, kp_ref, kc_ref, vp_ref, vc_ref = refs[:5]
    refs = refs[5:]
    if not first:
        oin_refs, lin_refs = refs[:MERGE_FAN], refs[MERGE_FAN:2 * MERGE_FAN]
        refs = refs[2 * MERGE_FAN:]
    out_refs = refs
    o_ref = out_refs[0]
    n = pl.program_id(2)
    q = q_ref[...] * ATTN_SCALE
    kk = jnp.concatenate([kp_ref[...], kc_ref[...]], axis=0)
    vv = jnp.concatenate([vp_ref[...], vc_ref[...]], axis=0)
    qi = lax.broadcasted_iota(jnp.int32, (NK, 2 * NK), 0)
    kj = lax.broadcasted_iota(jnp.int32, (NK, 2 * NK), 1)
    dist = qi + NK - kj
    mask = (dist >= 0) & (dist <= NK) & ((kj >= NK) | (n > 0))
    lane = lax.broadcasted_iota(jnp.int32, (NK, LANES), 1)
    lane_first = lane < HEAD_DIM
    lse_out = jnp.zeros((NK, LANES), F32)
    o_parts = []
    for pr in range(HPG // 2):
        sl = slice(pr * LANES, (pr + 1) * LANES)
        q_pair, k_pair, v_pair = q[:, sl], kk[:, sl], vv[:, sl]
        res = []
        for sub in range(2):
            h = pr * 2 + sub
            qm = jnp.where(lane_first if sub == 0 else ~lane_first, q_pair, jnp.zeros_like(q_pair))
            s = jnp.where(mask, _dot_nt(qm, k_pair), -jnp.inf)
            m = jnp.max(s, axis=-1, keepdims=True)
            p = jnp.exp(s - m)
            l = jnp.sum(p, axis=-1, keepdims=True)
            res.append(_dot(p.astype(BF16), v_pair) / l)
            lse_out = jnp.where(lane == h, m + jnp.log(l), lse_out)
        o_parts.append(jnp.where(lane_first, res[0], res[1]))
    o = jnp.concatenate(o_parts, axis=1)
    if not first:
        rows = NK // MERGE_FAN
        dst = lax.broadcasted_iota(jnp.int32, (NK, NK), 0)
        src = lax.broadcasted_iota(jnp.int32, (NK, NK), 1)
        perm = (src == (dst % MERGE_FAN) * rows + dst // MERGE_FAN).astype(BF16)
        o_prev = _dot(perm, jnp.concatenate([r[...] for r in oin_refs], axis=0))
        lse_prev = _dot_sel_l(perm, jnp.concatenate([r[...] for r in lin_refs], axis=0))
        lse_new = jnp.maximum(lse_out, lse_prev) + jnp.log1p(jnp.exp(-jnp.abs(lse_out - lse_prev)))
        expand = _head_expand(LANES, HPG)
        w_prev = _dot_sel_r(jnp.exp(lse_prev - lse_new), expand)
        w_cur = _dot_sel_r(jnp.exp(lse_out - lse_new), expand)
        o = o_prev * w_prev + o * w_cur
        lse_out = lse_new
    o_ref[...] = o.astype(o_ref.dtype)
    if not last:
        out_refs[1][...] = lse_out


def _attn_prompt_group(q, k, v, batch, gi, first, last, o_in, lse_in):
    window, dil = DIL_GROUPS[gi]
    td = q.shape[0]
    sd = td // batch
    assert window // dil == NK and sd % NK == 0 and q.shape[1] == dil * GROUP_WIDTH
    nb = sd // NK
    blk = lambda width: pl.BlockSpec((NK, width), lambda i, r, n: (i * nb + n, r))
    prev = pl.BlockSpec((NK, GROUP_WIDTH), lambda i, r, n: (i * nb + jnp.maximum(n - 1, 0), r))
    args, in_specs = [q, k, k, v, v], [blk(GROUP_WIDTH), prev, blk(GROUP_WIDTH), prev, blk(GROUP_WIDTH)]
    if not first:
        assert o_in.shape == (td // MERGE_FAN, MERGE_FAN * dil * GROUP_WIDTH)
        piece = lambda width, c: pl.BlockSpec((NK // MERGE_FAN, width), lambda i, r, n: (i * nb + n, dil * c + r))
        args += [o_in] * MERGE_FAN + [lse_in] * MERGE_FAN
        in_specs += [piece(GROUP_WIDTH, c) for c in range(MERGE_FAN)] + [piece(LANES, c) for c in range(MERGE_FAN)]
    out_specs = [blk(GROUP_WIDTH)]
    out_shape = [jax.ShapeDtypeStruct((td, dil * GROUP_WIDTH), BF16)]
    if not last:
        out_specs.append(blk(LANES))
        out_shape.append(jax.ShapeDtypeStruct((td, dil * LANES), F32))
    vmem = (2 * 7 * _nbytes((NK, GROUP_WIDTH), BF16) + 6 * _nbytes((NK, LANES), F32)
            + 24 * _nbytes((NK, 2 * NK), F32) + 6 * _nbytes((NK, GROUP_WIDTH), F32))
    outs = pl.pallas_call(
        functools.partial(_attn_prompt_kernel, first=first, last=last),
        grid=(batch, dil, nb),
        in_specs=in_specs, out_specs=out_specs, out_shape=out_shape,
        compiler_params=_cparams(("parallel", "parallel", "parallel"), vmem),
        name=f"attn_prompt_w{window}",
    )(*args)
    return outs[0], (None if last else outs[1])


def _out_kernel(h_ref, y_ref, at_ref, gs_ref, ga_ref, p_ref,
                wos_ref, woa_ref, wo_ref, n2_ref, wg_ref, wu_ref, wd_ref, np_ref, wpg_ref, wpp_ref, nf_ref,
                o_ref):
    merged = (gs_ref[...].astype(F32) * _dot(y_ref[...].astype(BF16), wos_ref[...])
              + ga_ref[...].astype(F32) * _dot(at_ref[...].astype(BF16), woa_ref[...]))
    h = h_ref[...] + _dot(merged.astype(BF16), wo_ref[...])
    h = _swiglu_half(h, n2_ref[...], wg_ref, wu_ref, wd_ref)
    gate = _sigmoid(_dot(_rms(h, np_ref[...]).astype(BF16), wpg_ref[...]))
    h = h + gate * _dot(p_ref[...].astype(BF16), wpp_ref[...])
    o_ref[...] = _rms(h, nf_ref[...])


def _out_stage(h, y, attn, gs, ga, p, weights, tm):
    t, d = h.shape
    row = lambda i: (i, 0)
    acts = [h, y, attn, gs, ga, p]
    vmem = (sum(_nbytes(w.shape, w.dtype) for w in weights)
            + sum(2 * _nbytes((tm, a.shape[1]), a.dtype) for a in acts) + 2 * _nbytes((tm, d), F32)
            + 8 * _nbytes((tm, d), F32) + 4 * _nbytes((tm, FF_CHUNK), F32))
    return pl.pallas_call(
        _out_kernel,
        grid=(t // tm,),
        in_specs=[pl.BlockSpec((tm, a.shape[1]), row) for a in acts] + [_resident(w.shape) for w in weights],
        out_specs=pl.BlockSpec((tm, d), row),
        out_shape=jax.ShapeDtypeStruct((t, d), F32),
        compiler_params=_cparams(("parallel",), vmem),
        name="output_stage",
    )(*acts, *weights)


def _ssd_sample_prep_kernel(xn_ref, sc_ref, dtr_ref, cw_ref, cb_ref, dtb_ref, alog_ref,
                            xs_ref, xdt_t_ref, bdt_ref, c_t_ref, dec_ref):
    conv_dim = xn_ref.shape[1]
    d_inner = xs_ref.shape[1]
    gn = SSM_GROUPS * SSM_STATE
    conv = cb_ref[...] + xn_ref[...] * cw_ref[CONV_W - 1:CONV_W, :]
    for tap in range(CONV_W - 1):
        conv = conv + sc_ref[:, tap * conv_dim:(tap + 1) * conv_dim] * cw_ref[tap:tap + 1, :]
    xa = _silu(conv)
    xs = xa[:, :d_inner]
    xs_ref[...] = xs
    dt = _softplus(dtr_ref[...] + dtb_ref[...])
    dec_ref[...] = jnp.exp(dt * -jnp.exp(alog_ref[...]))
    n_heads = d_inner // HEAD_DIM
    dt_x = _dot_sel_r(dt, _head_expand(LANES, n_heads))
    xdt_t_ref[...] = (xs * dt_x).T.astype(xdt_t_ref.dtype)
    bdt_ref[...] = xa[:, d_inner:d_inner + gn]
    for g in range(SSM_GROUPS):
        lo = d_inner + gn + g * SSM_STATE
        c_t_ref[g] = xa[:, lo:lo + SSM_STATE].T.astype(c_t_ref.dtype)


def _ssd_sample_prep(xn, sc, dtr, cw, cb, dtb, alog, d_inner):
    nb, conv_dim = xn.shape
    args = [xn, sc, dtr, cw, cb, dtb, alog]
    out_shape = [jax.ShapeDtypeStruct((nb, d_inner), F32),
                 jax.ShapeDtypeStruct((d_inner, nb), BF16),
                 jax.ShapeDtypeStruct((nb, SSM_GROUPS * SSM_STATE), F32),
                 jax.ShapeDtypeStruct((SSM_GROUPS, SSM_STATE, nb), BF16),
                 jax.ShapeDtypeStruct((nb, LANES), F32)]
    vmem = 2 * sum(_nbytes(a.shape, a.dtype) for a in args) + 2 * sum(_nbytes(o.shape, o.dtype) for o in out_shape) \
        + 8 * _nbytes((nb, conv_dim), F32)
    return pl.pallas_call(
        _ssd_sample_prep_kernel,
        grid=(1,),
        in_specs=[_resident(a.shape) for a in args],
        out_specs=[pl.BlockSpec(o.shape, functools.partial(lambda nd, i: (0,) * nd, len(o.shape))) for o in out_shape],
        out_shape=out_shape,
        compiler_params=_cparams(("arbitrary",), vmem),
        name="ssd_sample_prep",
    )(*args)


SAMPLE_STATE_TILE = 4


def _ssd_sample_state_kernel(dec_ref, st_ref, xdt_t_ref, bdt_ref, c_t_ref, sto_ref, y_t_ref):
    i = pl.program_id(0)
    nb = bdt_ref.shape[0]
    n_heads = st_ref.shape[1]
    gw = HPG * HEAD_DIM

    @pl.when(i == 0)
    def _():
        y_t_ref[...] = jnp.zeros(y_t_ref.shape, F32)

    row = lax.broadcasted_iota(jnp.int32, (nb, SSM_STATE), 0)
    lane = lax.broadcasted_iota(jnp.int32, (gw, nb), 1)
    for j in range(SAMPLE_STATE_TILE):
        b = i * SAMPLE_STATE_TILE + j
        for g in range(SSM_GROUPS):
            b_row = jnp.where(row == b, bdt_ref[:, g * SSM_STATE:(g + 1) * SSM_STATE], 0.0).astype(BF16)
            upd = _dot(xdt_t_ref[g * gw:(g + 1) * gw, :], b_row)
            new = []
            for hg in range(HPG):
                h = g * HPG + hg
                new_h = st_ref[j, h] * dec_ref[b * n_heads + h] + upd[hg * HEAD_DIM:(hg + 1) * HEAD_DIM, :]
                sto_ref[j, h] = new_h
                new.append(new_h)
            yg = _dot(jnp.concatenate(new, axis=0).astype(BF16), c_t_ref[g])
            y_t_ref[g * gw:(g + 1) * gw, :] += jnp.where(lane == b, yg, 0.0)


def _ssd_sample_state(dec, st, xdt_t, bdt, c_t):
    nb, n_heads = st.shape[0], st.shape[1]
    d_inner = n_heads * HEAD_DIM
    bt = SAMPLE_STATE_TILE
    st_spec = pl.BlockSpec((bt, n_heads, HEAD_DIM, SSM_STATE), lambda i: (i, 0, 0, 0))
    vmem = (4 * _nbytes((bt, n_heads, HEAD_DIM, SSM_STATE), F32) + _nbytes(xdt_t.shape, BF16)
            + _nbytes(bdt.shape, F32) + _nbytes(c_t.shape, BF16) + 2 * _nbytes((d_inner, nb), F32)
            + 16 * _nbytes((HPG * HEAD_DIM, SSM_STATE), F32))
    return pl.pallas_call(
        _ssd_sample_state_kernel,
        grid=(nb // bt,),
        in_specs=[pl.BlockSpec(memory_space=pltpu.SMEM), st_spec,
                  _resident(xdt_t.shape), _resident(bdt.shape), _resident(c_t.shape)],
        out_specs=[st_spec, pl.BlockSpec((d_inner, nb), lambda i: (0, 0))],
        out_shape=[jax.ShapeDtypeStruct(st.shape, F32), jax.ShapeDtypeStruct((d_inner, nb), F32)],
        compiler_params=_cparams(("arbitrary",), vmem),
        name="ssd_sample_state",
    )(dec, st, xdt_t, bdt, c_t)


def _ssd_sample_post_kernel(y_t_ref, xs_ref, z_ref, dskip_ref, ng_ref, y_ref):
    y = y_t_ref[...].T + dskip_ref[...] * xs_ref[...]
    y_ref[...] = _ssd_gate_norm(y, z_ref[...], ng_ref[...]).astype(y_ref.dtype)


def _ssd_sample_post(y_t, xs, z, dskip, ng):
    args = [y_t, xs, z, dskip, ng]
    vmem = 2 * sum(_nbytes(a.shape, a.dtype) for a in args) + 8 * _nbytes(xs.shape, F32)
    return pl.pallas_call(
        _ssd_sample_post_kernel,
        grid=(1,),
        in_specs=[_resident(a.shape) for a in args],
        out_specs=pl.BlockSpec(xs.shape, lambda i: (0, 0)),
        out_shape=jax.ShapeDtypeStruct(xs.shape, F32),
        compiler_params=_cparams(("arbitrary",), vmem),
        name="ssd_sample_post",
    )(*args)


def _attn_sample_kernel(q_ref, qt_ref, k_ref, v_ref, c0_ref, c1_ref, c2_ref, o_ref):
    q_rows = q_ref[0] * ATTN_SCALE
    q_cols = qt_ref[0] * ATTN_SCALE
    s_new_all = jnp.sum(k_ref[0] * q_rows, axis=-1, keepdims=True)
    scores = []
    for gi, (c_ref, (_, dil)) in enumerate(zip((c0_ref, c1_ref, c2_ref), DIL_GROUPS)):
        rows = []
        for h in range(HPG):
            col = gi * HPG + h
            rows.append(jnp.sum(c_ref[0, 0, h] * q_cols[:, col:col + 1], axis=0, keepdims=True))
        s = jnp.concatenate(rows, axis=0)
        pos = lax.broadcasted_iota(jnp.int32, s.shape, 1)
        scores.append((jnp.where(pos % dil == 0, s, -jnp.inf), s_new_all[gi * HPG:(gi + 1) * HPG]))
    m = functools.reduce(jnp.maximum, [jnp.maximum(jnp.max(s, axis=1, keepdims=True), sn) for s, sn in scores])
    den = jnp.zeros((HPG, 1), F32)
    o_new = jnp.zeros((HPG, HEAD_DIM), F32)
    o_cache_t = jnp.zeros((HEAD_DIM, LANES), F32)
    lane = lax.broadcasted_iota(jnp.int32, (HEAD_DIM, LANES), 1)
    for gi, ((s, sn), c_ref) in enumerate(zip(scores, (c0_ref, c1_ref, c2_ref))):
        p = jnp.exp(s - m)
        p_n = jnp.exp(sn - m)
        den = den + jnp.sum(p, axis=1, keepdims=True) + p_n
        o_new = o_new + p_n * v_ref[0, gi * HPG:(gi + 1) * HPG, :]
        for h in range(HPG):
            contrib = jnp.sum(c_ref[0, 1, h] * p[h:h + 1, :], axis=1, keepdims=True)
            o_cache_t = jnp.where(lane == h, o_cache_t + contrib, o_cache_t)
    o_cache = jnp.concatenate([o_cache_t, jnp.zeros_like(o_cache_t)], axis=0).T
    o_ref[0] = (o_cache[:HPG, :HEAD_DIM] + o_new) / den


def _attn_sample(q, k, v, caches):
    nb, n_heads, _ = q.shape
    views, specs = [], []
    for (window, dil), c in zip(DIL_GROUPS, caches):
        assert c.shape[1] == window and window // dil == NK
        views.append(jnp.transpose(c, (0, 2, 3, 4, 1)))
        specs.append(pl.BlockSpec((1, 2, HPG, HEAD_DIM, window), lambda i: (i, 0, 0, 0, 0)))
    vec = pl.BlockSpec((1, n_heads, HEAD_DIM), lambda i: (i, 0, 0))
    vec_t = pl.BlockSpec((1, HEAD_DIM, n_heads), lambda i: (i, 0, 0))
    cache_bytes = sum(_nbytes((2, HPG, HEAD_DIM, window), F32) for window, _ in DIL_GROUPS)
    vmem = 2 * cache_bytes + 6 * _nbytes((HEAD_DIM, DIL_GROUPS[-1][0]), F32) + 8 * _nbytes((LANES, LANES), F32)
    return pl.pallas_call(
        _attn_sample_kernel,
        grid=(nb,),
        in_specs=[vec, vec_t, vec, vec] + specs,
        out_specs=pl.BlockSpec((1, HPG, HEAD_DIM), lambda i: (i, 0, 0)),
        out_shape=jax.ShapeDtypeStruct((nb, HPG, HEAD_DIM), F32),
        compiler_params=_cparams(("parallel",), vmem),
        name="attn_sample",
    )(q, jnp.swapaxes(q, 1, 2), k, v, *views)


def _row(v, width=None):
    v = v.astype(F32).reshape(1, -1)
    if width is not None and v.shape[1] < width:
        v = jnp.pad(v, ((0, 0), (0, width - v.shape[1])))
    return v


def _layer_weights(prm):
    d_model = prm['w_in'].shape[0]
    d_inner = prm['norm_ssm'].shape[0]
    n_heads = d_inner // HEAD_DIM
    conv_dim = prm['conv_w'].shape[1]
    attn_w = len(DIL_GROUPS) * HPG * HEAD_DIM
    splits = [d_inner, conv_dim, n_heads, attn_w, attn_w, attn_w, d_model, d_model]
    offs = [0]
    for n in splits:
        offs.append(offs[-1] + n)
    w_in = prm['w_in']
    seg = [w_in[:, offs[j]:offs[j + 1]].astype(BF16) for j in range(len(splits))]
    seg[2] = jnp.pad(seg[2], ((0, 0), (0, LANES - n_heads)))
    bf = lambda name: prm[name].astype(BF16)
    return dict(
        ffn1=(_row(prm['norm_ffn1']), bf('w_ffn1_gate'), bf('w_ffn1_up'), bf('w_ffn1_down'), _row(prm['norm_mix'])),
        proj_ssm=seg[0:3], proj_attn=seg[3:8],
        conv_w=prm['conv_w'].astype(F32), conv_b=_row(prm['conv_b']),
        dt_bias=_row(prm['dt_bias'], LANES), a_log=_row(prm['a_log'], LANES),
        d_skip=_row(jnp.repeat(prm['d_skip'], HEAD_DIM)), norm_ssm=_row(prm['norm_ssm']),
        out=(bf('w_o_ssm'), bf('w_o_attn'), bf('w_out'), _row(prm['norm_ffn2']), bf('w_ffn2_gate'),
             bf('w_ffn2_up'), bf('w_ffn2_down'), _row(prm['norm_ple']), bf('w_ple_gate'), bf('w_ple_proj')),
        d_inner=d_inner, conv_dim=conv_dim,
    )


def _kv_stack(k, v, gi, keep):
    b, s, _ = k.shape
    gw = HPG * HEAD_DIM
    sel = lambda t: t[:, s - keep:, gi * gw:(gi + 1) * gw].astype(F32).reshape(b, keep, HPG, HEAD_DIM)
    return jnp.stack([sel(k), sel(v)], axis=2)


def _kv_stack_wide(k, v, batch, dil, keep):
    rows = k.shape[0] // batch
    sel = lambda t: (t.reshape(batch, rows, dil * GROUP_WIDTH)[:, rows - keep // dil:, :]
                     .astype(F32).reshape(batch, keep, HPG, HEAD_DIM))
    return jnp.stack([sel(k), sel(v)], axis=2)


def _prompt_layer(x, p_emb, w, norm_final, tm):
    b, s, d = x.shape
    t = b * s
    h1, u = _ffn(x.reshape(t, d), *w['ffn1'], tm)
    z, xbc, dt = _proj(_proj_ssm_kernel, u, w['proj_ssm'], (BF16, BF16, F32), tm, "proj_ssm")
    qs, ks, vs, gs, ga = _proj_attn_wide(u, w['proj_attn'], s, tm)
    r3 = lambda a: a.reshape(b, s, a.shape[-1])
    xbc3 = r3(xbc)
    y_ssm, ssm_new = _ssd_prompt(r3(z), xbc3, r3(dt), w['conv_w'], w['conv_b'], w['dt_bias'], w['a_log'],
                                 w['d_skip'], w['norm_ssm'])
    order = sorted(range(len(DIL_GROUPS)), key=lambda gi: -DIL_GROUPS[gi][1])
    assert all(DIL_GROUPS[a][1] == MERGE_FAN * DIL_GROUPS[c][1] for a, c in zip(order, order[1:]))
    assert DIL_GROUPS[order[-1]][1] == 1
    o, lse = None, None
    for j, gi in enumerate(order):
        o, lse = _attn_prompt_group(qs[gi], ks[gi], vs[gi], b, gi, j == 0, j == len(order) - 1, o, lse)
    y = _out_stage(h1, y_ssm.reshape(t, -1), o, gs, ga, p_emb.reshape(t, -1),
                   w['out'] + (_row(norm_final),), OUT_ROW_TILE)
    kv = [_kv_stack_wide(ks[gi], vs[gi], b, dil, min(window, s)) for gi, (window, dil) in enumerate(DIL_GROUPS)]
    conv_new = xbc3[:, s - (CONV_W - 1):, :].astype(F32)
    return y.reshape(b, s, d), kv, conv_new, ssm_new


def _sample_layer(x, p_emb, w, norm_final, conv_prev, ssm_prev, caches):
    nb, s, d = x.shape
    assert s == 1
    h1, u = _ffn(x.reshape(nb, d), *w['ffn1'], nb)
    z, xbc, dt = _proj(_proj_ssm_kernel, u, w['proj_ssm'], (F32, F32, F32), nb, "proj_ssm_sample")
    q, k, v, gs, ga = _proj(functools.partial(_proj_attn_kernel, seq=1, pos0=PAST_LEN), u, w['proj_attn'],
                            (F32,) * 5, nb, "proj_attn_sample")
    xs, xdt_t, bdt, c_t, dec = _ssd_sample_prep(xbc, conv_prev.reshape(nb, -1), dt, w['conv_w'], w['conv_b'],
                                                w['dt_bias'], w['a_log'], w['d_inner'])
    dec_flat = dec[:, :ssm_prev.shape[1]].reshape(-1)
    ssm_new, y_t = _ssd_sample_state(dec_flat, ssm_prev, xdt_t, bdt, c_t)
    y_ssm = _ssd_sample_post(y_t, xs, z, w['d_skip'], w['norm_ssm'])
    by_head = lambda a: a.reshape(nb, -1, HEAD_DIM)
    attn = _attn_sample(by_head(q), by_head(k), by_head(v), caches)
    y = _out_stage(h1, y_ssm, attn.reshape(nb, -1), gs, ga, p_emb.reshape(nb, -1),
                   w['out'] + (_row(norm_final),), nb)
    r3 = lambda a: a.reshape(nb, 1, a.shape[-1])
    kv = [_kv_stack(r3(k), r3(v), gi, 1) for gi in range(len(DIL_GROUPS))]
    conv_new = jnp.concatenate([conv_prev[:, 1:], xbc[:, None, :]], axis=1)
    return y.reshape(nb, 1, d), kv, conv_new, ssm_new


PROMPT_ROW_TILE = 512
OUT_ROW_TILE = 512


def kernel(x_prompt, x_sample, cache_kv_w128, cache_kv_w512, cache_kv_w2048, state_conv, state_ssm, p_prompt, p_sample, norm_ffn1, w_ffn1_gate, w_ffn1_up, w_ffn1_down, norm_mix, w_in, conv_w, conv_b, dt_bias, a_log, d_skip, norm_ssm, w_o_ssm, w_o_attn, w_out, norm_ffn2, w_ffn2_gate, w_ffn2_up, w_ffn2_down, norm_ple, w_ple_gate, w_ple_proj, norm_final):
    depth = w_in.shape[0]
    assert depth == 1
    layer_params = dict(
        norm_ffn1=norm_ffn1, w_ffn1_gate=w_ffn1_gate, w_ffn1_up=w_ffn1_up, w_ffn1_down=w_ffn1_down,
        norm_mix=norm_mix, w_in=w_in, conv_w=conv_w, conv_b=conv_b, dt_bias=dt_bias, a_log=a_log,
        d_skip=d_skip, norm_ssm=norm_ssm, w_o_ssm=w_o_ssm, w_o_attn=w_o_attn, w_out=w_out,
        norm_ffn2=norm_ffn2, w_ffn2_gate=w_ffn2_gate, w_ffn2_up=w_ffn2_up, w_ffn2_down=w_ffn2_down,
        norm_ple=norm_ple, w_ple_gate=w_ple_gate, w_ple_proj=w_ple_proj)
    i = 0
    w = _layer_weights({name: val[i] for name, val in layer_params.items()})
    yp, kvp, convp, ssmp = _prompt_layer(x_prompt, p_prompt[i], w, norm_final, PROMPT_ROW_TILE)
    ys, kvs, convs, ssms = _sample_layer(x_sample, p_sample[i], w, norm_final, state_conv[i], state_ssm[i],
                                         (cache_kv_w128[i], cache_kv_w512[i], cache_kv_w2048[i]))
    st = lambda a: a[None]
    return (yp, ys, st(kvp[0]), st(kvp[1]), st(kvp[2]), st(convp), st(ssmp),
            st(kvs[0]), st(kvs[1]), st(kvs[2]), st(convs), st(ssms))
```

```python
import functools
import math

import jax
import jax.numpy as jnp
from jax import lax
from jax.experimental import pallas as pl
from jax.experimental.pallas import tpu as pltpu

F32 = jnp.float32
BF16 = jnp.bfloat16

EPS = 1e-6
ROPE_THETA = 10000.0
PAST_LEN = 8192
HEAD_DIM = 64
HPG = 8
SSM_GROUPS = 4
SSM_STATE = 128
CONV_W = 4
DIL_GROUPS = ((128, 1), (512, 4), (2048, 16))
ATTN_SCALE = HEAD_DIM ** -0.5
NK = 128

LANES = 128
SUBLANES = 8
VMEM_LIMIT_CAP = 56 * 1024 * 1024


def _cparams(sem, vmem_bytes):
    return pltpu.CompilerParams(dimension_semantics=sem,
                                vmem_limit_bytes=int(min(VMEM_LIMIT_CAP, vmem_bytes)))


def _resident(shape):
    nd = len(shape)
    return pl.BlockSpec(shape, lambda *_: (0,) * nd, pipeline_mode=pl.Buffered(1))


def _nbytes(shape, dtype):
    return math.prod(shape) * jnp.dtype(dtype).itemsize


def _rms(x, gain):
    ms = jnp.mean(x * x, axis=-1, keepdims=True)
    return x * lax.rsqrt(ms + EPS) * gain


def _sigmoid(x):
    return 1.0 / (1.0 + jnp.exp(-x))


def _silu(x):
    return x * _sigmoid(x)


def _softplus(x):
    return jnp.maximum(x, 0.0) + jnp.log1p(jnp.exp(-jnp.abs(x)))


def _dot(a, b):
    return jnp.dot(a, b, preferred_element_type=F32)


def _dot_nt(a, b):
    return lax.dot_general(a, b, (((1,), (1,)), ((), ())), preferred_element_type=F32)


def _split3(x):
    hi = x.astype(BF16)
    r1 = x - hi.astype(F32)
    mid = r1.astype(BF16)
    lo = (r1 - mid.astype(F32)).astype(BF16)
    return hi, mid, lo


def _dot_sel_l(sel, x):
    hi, mid, lo = _split3(x)
    return _dot(sel, hi) + _dot(sel, mid) + _dot(sel, lo)


def _dot_sel_r(x, sel):
    hi, mid, lo = _split3(x)
    return _dot(hi, sel) + _dot(mid, sel) + _dot(lo, sel)


def _head_expand(n_heads_padded, n_heads):
    r = lax.broadcasted_iota(jnp.int32, (n_heads_padded, n_heads * HEAD_DIM), 0)
    c = lax.broadcasted_iota(jnp.int32, (n_heads_padded, n_heads * HEAD_DIM), 1)
    return (c // HEAD_DIM == r).astype(BF16)


FF_CHUNK = 256


def _swiglu_half(x, gain, wg_ref, wu_ref, wd_ref):
    u = _rms(x, gain).astype(BF16)
    d_ff = wg_ref.shape[1]
    acc = jnp.zeros(x.shape, F32)
    for c in range(d_ff // FF_CHUNK):
        sl = slice(c * FF_CHUNK, (c + 1) * FF_CHUNK)
        g = _dot(u, wg_ref[:, sl])
        up = _dot(u, wu_ref[:, sl])
        acc = acc + _dot((_silu(g) * up).astype(BF16), wd_ref[sl, :])
    return x + 0.5 * acc


def _ffn_kernel(x_ref, n1_ref, wg_ref, wu_ref, wd_ref, n2_ref, h_ref, u_ref):
    h = _swiglu_half(x_ref[...], n1_ref[...], wg_ref, wu_ref, wd_ref)
    h_ref[...] = h
    u_ref[...] = _rms(h, n2_ref[...]).astype(u_ref.dtype)


def _ffn(x, n1, wg, wu, wd, n2, tm):
    t, d = x.shape
    d_ff = wg.shape[1]
    vmem = (3 * _nbytes((d, d_ff), BF16) + 2 * 2 * _nbytes((tm, d), F32) + 2 * _nbytes((tm, d), BF16)
            + 6 * _nbytes((tm, d), F32) + 4 * _nbytes((tm, FF_CHUNK), F32))
    return pl.pallas_call(
        _ffn_kernel,
        grid=(t // tm,),
        in_specs=[pl.BlockSpec((tm, d), lambda i: (i, 0)),
                  _resident((1, d)), _resident((d, d_ff)), _resident((d, d_ff)), _resident((d_ff, d)),
                  _resident((1, d))],
        out_specs=[pl.BlockSpec((tm, d), lambda i: (i, 0)), pl.BlockSpec((tm, d), lambda i: (i, 0))],
        out_shape=[jax.ShapeDtypeStruct((t, d), F32), jax.ShapeDtypeStruct((t, d), BF16)],
        compiler_params=_cparams(("parallel",), vmem),
        name="ffn_half_step",
    )(x, n1, wg, wu, wd, n2)


def _rope_tables(tm, seq, pos0):
    row = lax.broadcasted_iota(jnp.int32, (tm, LANES), 0) + pl.program_id(0) * tm
    pos = (row % seq + pos0).astype(F32)
    lane = lax.broadcasted_iota(jnp.int32, (tm, LANES), 1)
    half = HEAD_DIM // 2
    j = (lane % half).astype(F32)
    inv_freq = jnp.exp(j * (-math.log(ROPE_THETA) / half))
    ang = pos * inv_freq
    first = (lane % HEAD_DIM) < half
    return jnp.cos(ang), jnp.where(first, -jnp.sin(ang), jnp.sin(ang)), first


def _rotary(x, cos, sin_signed, first):
    outs = []
    for c in range(x.shape[1] // LANES):
        xb = x[:, c * LANES:(c + 1) * LANES]
        other = jnp.where(first, pltpu.roll(xb, LANES - HEAD_DIM // 2, 1), pltpu.roll(xb, HEAD_DIM // 2, 1))
        outs.append(xb * cos + other * sin_signed)
    return jnp.concatenate(outs, axis=1)


def _proj_ssm_kernel(u_ref, wz_ref, wx_ref, wdt_ref, z_ref, xbc_ref, dt_ref):
    u = u_ref[...]
    z_ref[...] = _dot(u, wz_ref[...]).astype(z_ref.dtype)
    xbc_ref[...] = _dot(u, wx_ref[...]).astype(xbc_ref.dtype)
    dt_ref[...] = _dot(u, wdt_ref[...])


def _proj_attn_kernel(u_ref, wq_ref, wk_ref, wv_ref, wgs_ref, wga_ref,
                      q_ref, k_ref, v_ref, gs_ref, ga_ref, *, seq, pos0):
    u = u_ref[...]
    cos, sin_signed, first = _rope_tables(u.shape[0], seq, pos0)
    q_ref[...] = _rotary(_dot(u, wq_ref[...]), cos, sin_signed, first).astype(q_ref.dtype)
    k_ref[...] = _rotary(_dot(u, wk_ref[...]), cos, sin_signed, first).astype(k_ref.dtype)
    v_ref[...] = _dot(u, wv_ref[...]).astype(v_ref.dtype)
    gs_ref[...] = _sigmoid(_dot(u, wgs_ref[...])).astype(gs_ref.dtype)
    ga_ref[...] = _sigmoid(_dot(u, wga_ref[...])).astype(ga_ref.dtype)


GROUP_WIDTH = HPG * HEAD_DIM
_STAGED_GROUPS = [gi for gi, (_, dil) in enumerate(DIL_GROUPS) if dil > 1]


def _store_wide(stage, val, out_ref, dil):
    if dil == 1:
        out_ref[...] = val.astype(out_ref.dtype)
        return
    rows = val.shape[0] // dil
    for j in range(GROUP_WIDTH // LANES):
        stage[j] = val[:, j * LANES:(j + 1) * LANES]
    for r in range(dil):
        for j in range(GROUP_WIDTH // LANES):
            lo = r * GROUP_WIDTH + j * LANES
            out_ref[:, lo:lo + LANES] = stage[j, pl.ds(r, rows, stride=dil), :].astype(out_ref.dtype)


def _rope_table_kernel(cos_ref, sin_ref, *, seq):
    cos, sin_signed, _ = _rope_tables(cos_ref.shape[0], seq, 0)
    cos_ref[...] = cos
    sin_ref[...] = sin_signed


def _rope_table(seq, tm):
    spec = pl.BlockSpec((tm, LANES), lambda i: (i, 0))
    shape = jax.ShapeDtypeStruct((seq, LANES), F32)
    return pl.pallas_call(
        functools.partial(_rope_table_kernel, seq=seq),
        grid=(seq // tm,), in_specs=[], out_specs=[spec, spec], out_shape=[shape, shape],
        compiler_params=_cparams(("parallel",), 16 * _nbytes((tm, LANES), F32)),
        name="rope_table",
    )()


def _proj_attn_wide_kernel(u_ref, cos_ref, sin_ref, wq_ref, wk_ref, wv_ref, wgs_ref, wga_ref, *refs):
    n_grp = len(DIL_GROUPS)
    q_refs, k_refs, v_refs = refs[0:n_grp], refs[n_grp:2 * n_grp], refs[2 * n_grp:3 * n_grp]
    gs_ref, ga_ref, stage = refs[3 * n_grp:]
    u = u_ref[...]
    cos, sin_signed = cos_ref[...], sin_ref[...]
    first = lax.broadcasted_iota(jnp.int32, cos.shape, 1) % HEAD_DIM < HEAD_DIM // 2
    for ti, (w_ref, out_refs, rope) in enumerate(((wq_ref, q_refs, True), (wk_ref, k_refs, True),
                                                 (wv_ref, v_refs, False))):
        res = _dot(u, w_ref[...])
        if rope:
            res = _rotary(res, cos, sin_signed, first)
        for gi, (_, dil) in enumerate(DIL_GROUPS):
            slot = None if dil == 1 else stage.at[ti * len(_STAGED_GROUPS) + _STAGED_GROUPS.index(gi)]
            _store_wide(slot, res[:, gi * GROUP_WIDTH:(gi + 1) * GROUP_WIDTH], out_refs[gi], dil)
    gs_ref[...] = _sigmoid(_dot(u, wgs_ref[...])).astype(gs_ref.dtype)
    ga_ref[...] = _sigmoid(_dot(u, wga_ref[...])).astype(ga_ref.dtype)


def _proj_attn_wide(u, weights, seq, tm):
    t, d = u.shape
    d_model = weights[3].shape[1]
    row = lambda i: (i, 0)
    qkv_specs = [pl.BlockSpec((tm // dil, dil * GROUP_WIDTH), row) for _, dil in DIL_GROUPS] * 3
    qkv_shapes = [jax.ShapeDtypeStruct((t // dil, dil * GROUP_WIDTH), BF16) for _, dil in DIL_GROUPS] * 3
    n_stage = 3 * len(_STAGED_GROUPS)
    vmem = (sum(_nbytes(w.shape, BF16) for w in weights) + 2 * _nbytes((tm, d), BF16)
            + 2 * _nbytes((tm, 3 * len(DIL_GROUPS) * GROUP_WIDTH + 2 * d_model), BF16)
            + 4 * _nbytes((tm, LANES), F32)
            + n_stage * _nbytes((tm, GROUP_WIDTH), F32) + 6 * _nbytes((tm, weights[0].shape[1]), F32))
    cos, sin_signed = _rope_table(seq, tm)
    tiles_per_seq = seq // tm
    table = pl.BlockSpec((tm, LANES), lambda i: (i % tiles_per_seq, 0))
    outs = pl.pallas_call(
        _proj_attn_wide_kernel,
        grid=(t // tm,),
        in_specs=[pl.BlockSpec((tm, d), row), table, table] + [_resident(w.shape) for w in weights],
        out_specs=qkv_specs + [pl.BlockSpec((tm, d_model), row)] * 2,
        out_shape=qkv_shapes + [jax.ShapeDtypeStruct((t, d_model), BF16)] * 2,
        scratch_shapes=[pltpu.VMEM((n_stage, GROUP_WIDTH // LANES, tm, LANES), F32)],
        compiler_params=_cparams(("parallel",), vmem),
        name="proj_attn",
    )(u, cos, sin_signed, *weights)
    n_grp = len(DIL_GROUPS)
    return outs[0:n_grp], outs[n_grp:2 * n_grp], outs[2 * n_grp:3 * n_grp], outs[3 * n_grp], outs[3 * n_grp + 1]


def _proj(kernel, u, weights, out_dtypes, tm, name):
    t, d = u.shape
    row = lambda i: (i, 0)
    widths = [w.shape[1] for w in weights]
    vmem = (sum(_nbytes(w.shape, BF16) for w in weights) + 2 * _nbytes((tm, d), BF16)
            + sum(2 * _nbytes((tm, n), dt) for n, dt in zip(widths, out_dtypes))
            + 4 * _nbytes((tm, max(widths)), F32))
    return pl.pallas_call(
        kernel,
        grid=(t // tm,),
        in_specs=[pl.BlockSpec((tm, d), row)] + [_resident(w.shape) for w in weights],
        out_specs=[pl.BlockSpec((tm, n), row) for n in widths],
        out_shape=[jax.ShapeDtypeStruct((t, n), dt) for n, dt in zip(widths, out_dtypes)],
        compiler_params=_cparams(("parallel",), vmem),
        name=name,
    )(u, *weights)


SSD_CHUNK = 128


def _ssd_gate_norm(y, z, gain):
    return _rms(y * _silu(z), gain)


def _ssd_prompt_kernel(z_ref, xprev_ref, xbc_ref, dt_ref, cw_ref, cb_ref, dtb_ref, alog_ref, dskip_ref, ng_ref,
                       shift_ref, expand_ref, y_ref, st_ref, st_t):
    L = SSD_CHUNK
    c = pl.program_id(1)
    d_inner = z_ref.shape[-1]
    gw = HPG * HEAD_DIM
    gn = SSM_GROUPS * SSM_STATE

    @pl.when(c == 0)
    def _():
        st_t[...] = jnp.zeros(st_t.shape, F32)

    x_cur = xbc_ref[0]
    x_prev = jnp.where(c > 0, xprev_ref[0], jnp.zeros_like(x_cur))
    both = jnp.concatenate([x_prev, x_cur], axis=0)
    conv = cb_ref[...] + x_cur.astype(F32) * cw_ref[CONV_W - 1:CONV_W, :]
    shifted = _dot(shift_ref[...], both)
    for back in range(1, CONV_W):
        tap = CONV_W - 1 - back
        conv = conv + shifted[(back - 1) * L:back * L, :] * cw_ref[tap:tap + 1, :]
    xa = _silu(conv)
    xs = xa[:, :d_inner]
    bm = xa[:, d_inner:d_inner + gn]
    cm = xa[:, d_inner + gn:]

    dt = _softplus(dt_ref[0] + dtb_ref[...])
    a = -jnp.exp(alog_ref[...])
    row = lax.broadcasted_iota(jnp.int32, (L, L), 0)
    col = lax.broadcasted_iota(jnp.int32, (L, L), 1)
    causal = row >= col
    acs = _dot_sel_l(causal.astype(BF16), dt * a)
    acs_t = acs.T
    dt_t = dt.T
    acs_last = acs[L - 1:L, :]

    expand = expand_ref[...]
    per_head = jnp.concatenate([jnp.exp(acs), jnp.exp(acs_last - acs) * dt], axis=0)
    per_lane = _dot(per_head.astype(BF16), expand)
    e_acs, w_end = per_lane[:L], per_lane[L:]
    dec_state = _dot_sel_r(jnp.broadcast_to(jnp.exp(acs_last), (SUBLANES, LANES)), expand)[0:1, :]

    xs_b = xs.astype(BF16)
    xw = (xs * w_end).astype(BF16)
    lane_first = lax.broadcasted_iota(jnp.int32, (L, LANES), 1) < HEAD_DIM
    y_groups = []
    for g in range(SSM_GROUPS):
        gsl = slice(g * gw, (g + 1) * gw)
        b_g = bm[:, g * SSM_STATE:(g + 1) * SSM_STATE]
        c_g = cm[:, g * SSM_STATE:(g + 1) * SSM_STATE].astype(BF16)
        cb = _dot_nt(c_g, b_g.astype(BF16))
        st_g = st_t[g]
        y_inter = _dot(c_g, st_g.astype(BF16))
        pairs = []
        for pr in range(HPG // 2):
            w_pair = []
            for sub in range(2):
                h = g * HPG + pr * 2 + sub
                seg = acs[:, h:h + 1] - acs_t[h:h + 1, :]
                w = cb * jnp.exp(jnp.where(causal, seg, -jnp.inf)) * dt_t[h:h + 1, :]
                w_pair.append(w.astype(BF16))
            lo = g * gw + pr * LANES
            res = _dot(jnp.concatenate(w_pair, axis=0), xs_b[:, lo:lo + LANES])
            pairs.append(jnp.where(lane_first, res[:L], res[L:]))
        y_groups.append(jnp.concatenate(pairs, axis=1) + y_inter * e_acs[:, gsl])
        st_t[g] = dec_state[:, gsl] * st_g + _dot(b_g.T.astype(BF16), xw[:, gsl])
    y = jnp.concatenate(y_groups, axis=1) + dskip_ref[...] * xs
    y_ref[0] = _ssd_gate_norm(y, z_ref[0].astype(F32), ng_ref[...]).astype(y_ref.dtype)

    @pl.when(c == pl.num_programs(1) - 1)
    def _():
        for g in range(SSM_GROUPS):
            st_ref[0, g * HPG:(g + 1) * HPG] = st_t[g].T.reshape(HPG, HEAD_DIM, SSM_STATE)


def _ssd_prompt(z, xbc, dt, cw, cb, dtb, alog, dskip, ng):
    b, s, d_inner = z.shape
    conv_dim = xbc.shape[-1]
    n_heads = d_inner // HEAD_DIM
    L = SSD_CHUNK
    blk = lambda n: pl.BlockSpec((1, L, n), lambda i, c: (i, c, 0))
    r = jnp.arange(L)[:, None]
    cidx = jnp.arange(2 * L)[None, :]
    shift = jnp.concatenate([(cidx == r + L - k) for k in range(1, CONV_W)], axis=0).astype(BF16)
    heads = jnp.arange(LANES)[:, None]
    expand = (jnp.arange(d_inner)[None, :] // HEAD_DIM == heads).astype(BF16)
    prev_blk = pl.BlockSpec((1, L, conv_dim), lambda i, c: (i, jnp.maximum(c - 1, 0), 0))
    vmem = (2 * (_nbytes((L, d_inner), z.dtype) * 2 + 2 * _nbytes((L, conv_dim), xbc.dtype) + _nbytes((L, LANES), F32))
            + 2 * _nbytes((n_heads, HEAD_DIM, SSM_STATE), F32)
            + _nbytes((SSM_GROUPS, SSM_STATE, HPG * HEAD_DIM), F32)
            + _nbytes(shift.shape, BF16) + _nbytes(expand.shape, BF16)
            + 16 * _nbytes((L, conv_dim), F32))
    return pl.pallas_call(
        _ssd_prompt_kernel,
        grid=(b, s // L),
        in_specs=[blk(d_inner), prev_blk, blk(conv_dim), blk(LANES),
                  _resident(cw.shape), _resident(cb.shape), _resident(dtb.shape), _resident(alog.shape),
                  _resident(dskip.shape), _resident(ng.shape), _resident(shift.shape), _resident(expand.shape)],
        out_specs=[blk(d_inner),
                   pl.BlockSpec((1, n_heads, HEAD_DIM, SSM_STATE), lambda i, c: (i, 0, 0, 0))],
        out_shape=[jax.ShapeDtypeStruct((b, s, d_inner), BF16),
                   jax.ShapeDtypeStruct((b, n_heads, HEAD_DIM, SSM_STATE), F32)],
        scratch_shapes=[pltpu.VMEM((SSM_GROUPS, SSM_STATE, HPG * HEAD_DIM), F32)],
        compiler_params=_cparams(("parallel", "arbitrary"), vmem),
        name="ssd_prompt_scan",
    )(z, xbc, xbc, dt, cw, cb, dtb, alog, dskip, ng, shift, expand)


MERGE_FAN = 4
ATTN_BLOCKS_PER_STEP = 4


def _attn_prompt_kernel(*refs, first, last):
    q_ref, kp_ref, kc_ref, vp_ref, vc_ref = refs[:5]
    refs = refs[5:]
    if not first:
        oin_refs, lin_refs = refs[:MERGE_FAN], refs[MERGE_FAN:2 * MERGE_FAN]
        refs = refs[2 * MERGE_FAN:]
    out_refs = refs
    o_ref = out_refs[0]
    n = pl.program_id(2)
    n_blocks = q_ref.shape[0] // NK
    qi = lax.broadcasted_iota(jnp.int32, (NK, 2 * NK), 0)
    kj = lax.broadcasted_iota(jnp.int32, (NK, 2 * NK), 1)
    dist = qi + NK - kj
    band = (dist >= 0) & (dist <= NK)
    lane = lax.broadcasted_iota(jnp.int32, (NK, LANES), 1)
    lane_first = lane < HEAD_DIM
    if not first:
        rows = NK // MERGE_FAN
        dst = lax.broadcasted_iota(jnp.int32, (NK, NK), 0)
        src = lax.broadcasted_iota(jnp.int32, (NK, NK), 1)
        perm = (src == (dst % MERGE_FAN) * rows + dst // MERGE_FAN).astype(BF16)
        expand = _head_expand(LANES, HPG)
    for j in range(n_blocks):
        own = slice(j * NK, (j + 1) * NK)
        q = q_ref[own, :] * ATTN_SCALE
        if j == 0:
            k_before, v_before = kp_ref[...], vp_ref[...]
            mask = band & ((kj >= NK) | (n > 0))
        else:
            before = slice((j - 1) * NK, j * NK)
            k_before, v_before = kc_ref[before, :], vc_ref[before, :]
            mask = band
        kk = jnp.concatenate([k_before, kc_ref[own, :]], axis=0)
        vv = jnp.concatenate([v_before, vc_ref[own, :]], axis=0)
        lse_out = jnp.zeros((NK, LANES), F32)
        o_parts = []
        for pr in range(HPG // 2):
            sl = slice(pr * LANES, (pr + 1) * LANES)
            q_pair, k_pair, v_pair = q[:, sl], kk[:, sl], vv[:, sl]
            res = []
            for sub in range(2):
                h = pr * 2 + sub
                qm = jnp.where(lane_first if sub == 0 else ~lane_first, q_pair, jnp.zeros_like(q_pair))
                s = jnp.where(mask, _dot_nt(qm, k_pair), -jnp.inf)
                m = jnp.max(s, axis=-1, keepdims=True)
                p = jnp.exp(s - m)
                l = jnp.sum(p, axis=-1, keepdims=True)
                res.append(_dot(p.astype(BF16), v_pair) / l)
                lse_out = jnp.where(lane == h, m + jnp.log(l), lse_out)
            o_parts.append(jnp.where(lane_first, res[0], res[1]))
        o = jnp.concatenate(o_parts, axis=1)
        if not first:
            part = slice(j * rows, (j + 1) * rows)
            o_prev = _dot(perm, jnp.concatenate([r[part, :] for r in oin_refs], axis=0))
            lse_prev = _dot_sel_l(perm, jnp.concatenate([r[part, :] for r in lin_refs], axis=0))
            lse_new = jnp.maximum(lse_out, lse_prev) + jnp.log1p(jnp.exp(-jnp.abs(lse_out - lse_prev)))
            w_prev = _dot_sel_r(jnp.exp(lse_prev - lse_new), expand)
            w_cur = _dot_sel_r(jnp.exp(lse_out - lse_new), expand)
            o = o_prev * w_prev + o * w_cur
            lse_out = lse_new
        o_ref[own, :] = o.astype(o_ref.dtype)
        if not last:
            out_refs[1][own, :] = lse_out


def _attn_prompt_group(q, k, v, batch, gi, first, last, o_in, lse_in):
    window, dil = DIL_GROUPS[gi]
    td = q.shape[0]
    sd = td // batch
    assert window // dil == NK and sd % NK == 0 and q.shape[1] == dil * GROUP_WIDTH
    nb = sd // NK
    g = math.gcd(nb, ATTN_BLOCKS_PER_STEP)
    ns = nb // g
    blk = lambda width: pl.BlockSpec((g * NK, width), lambda i, r, n: (i * ns + n, r))
    prev = pl.BlockSpec((NK, GROUP_WIDTH), lambda i, r, n: (i * nb + jnp.maximum(n * g - 1, 0), r))
    args, in_specs = [q, k, k, v, v], [blk(GROUP_WIDTH), prev, blk(GROUP_WIDTH), prev, blk(GROUP_WIDTH)]
    if not first:
        assert o_in.shape == (td // MERGE_FAN, MERGE_FAN * dil * GROUP_WIDTH)
        piece = lambda width, c: pl.BlockSpec((g * NK // MERGE_FAN, width),
                                              lambda i, r, n: (i * ns + n, dil * c + r))
        args += [o_in] * MERGE_FAN + [lse_in] * MERGE_FAN
        in_specs += [piece(GROUP_WIDTH, c) for c in range(MERGE_FAN)] + [piece(LANES, c) for c in range(MERGE_FAN)]
    out_specs = [blk(GROUP_WIDTH)]
    out_shape = [jax.ShapeDtypeStruct((td, dil * GROUP_WIDTH), BF16)]
    if not last:
        out_specs.append(blk(LANES))
        out_shape.append(jax.ShapeDtypeStruct((td, dil * LANES), F32))
    vmem = (2 * (6 * g + 2) * _nbytes((NK, GROUP_WIDTH), BF16) + 6 * g * _nbytes((NK, LANES), F32)
            + 24 * _nbytes((NK, 2 * NK), F32) + 6 * g * _nbytes((NK, GROUP_WIDTH), F32))
    outs = pl.pallas_call(
        functools.partial(_attn_prompt_kernel, first=first, last=last),
        grid=(batch, dil, ns),
        in_specs=in_specs, out_specs=out_specs, out_shape=out_shape,
        compiler_params=_cparams(("parallel", "parallel", "parallel"), vmem),
        name=f"attn_prompt_w{window}",
    )(*args)
    return outs[0], (None if last else outs[1])


def _out_kernel(h_ref, y_ref, at_ref, gs_ref, ga_ref, p_ref,
                wos_ref, woa_ref, wo_ref, n2_ref, wg_ref, wu_ref, wd_ref, np_ref, wpg_ref, wpp_ref, nf_ref,
                o_ref):
    merged = (gs_ref[...].astype(F32) * _dot(y_ref[...].astype(BF16), wos_ref[...])
              + ga_ref[...].astype(F32) * _dot(at_ref[...].astype(BF16), woa_ref[...]))
    h = h_ref[...] + _dot(merged.astype(BF16), wo_ref[...])
    h = _swiglu_half(h, n2_ref[...], wg_ref, wu_ref, wd_ref)
    gate = _sigmoid(_dot(_rms(h, np_ref[...]).astype(BF16), wpg_ref[...]))
    h = h + gate * _dot(p_ref[...].astype(BF16), wpp_ref[...])
    o_ref[...] = _rms(h, nf_ref[...])


def _out_stage(h, y, attn, gs, ga, p, weights, tm):
    t, d = h.shape
    row = lambda i: (i, 0)
    acts = [h, y, attn, gs, ga, p]
    vmem = (sum(_nbytes(w.shape, w.dtype) for w in weights)
            + sum(2 * _nbytes((tm, a.shape[1]), a.dtype) for a in acts) + 2 * _nbytes((tm, d), F32)
            + 8 * _nbytes((tm, d), F32) + 4 * _nbytes((tm, FF_CHUNK), F32))
    return pl.pallas_call(
        _out_kernel,
        grid=(t // tm,),
        in_specs=[pl.BlockSpec((tm, a.shape[1]), row) for a in acts] + [_resident(w.shape) for w in weights],
        out_specs=pl.BlockSpec((tm, d), row),
        out_shape=jax.ShapeDtypeStruct((t, d), F32),
        compiler_params=_cparams(("parallel",), vmem),
        name="output_stage",
    )(*acts, *weights)


def _ssd_sample_prep_kernel(xn_ref, sc_ref, dtr_ref, cw_ref, cb_ref, dtb_ref, alog_ref,
                            xs_ref, xdt_t_ref, bdt_ref, c_t_ref, dec_ref):
    conv_dim = xn_ref.shape[1]
    d_inner = xs_ref.shape[1]
    gn = SSM_GROUPS * SSM_STATE
    conv = cb_ref[...] + xn_ref[...] * cw_ref[CONV_W - 1:CONV_W, :]
    for tap in range(CONV_W - 1):
        conv = conv + sc_ref[:, tap * conv_dim:(tap + 1) * conv_dim] * cw_ref[tap:tap + 1, :]
    xa = _silu(conv)
    xs = xa[:, :d_inner]
    xs_ref[...] = xs
    dt = _softplus(dtr_ref[...] + dtb_ref[...])
    dec_ref[...] = jnp.exp(dt * -jnp.exp(alog_ref[...]))
    n_heads = d_inner // HEAD_DIM
    dt_x = _dot_sel_r(dt, _head_expand(LANES, n_heads))
    xdt_t_ref[...] = (xs * dt_x).T.astype(xdt_t_ref.dtype)
    bdt_ref[...] = xa[:, d_inner:d_inner + gn]
    for g in range(SSM_GROUPS):
        lo = d_inner + gn + g * SSM_STATE
        c_t_ref[g] = xa[:, lo:lo + SSM_STATE].T.astype(c_t_ref.dtype)


def _ssd_sample_prep(xn, sc, dtr, cw, cb, dtb, alog, d_inner):
    nb, conv_dim = xn.shape
    args = [xn, sc, dtr, cw, cb, dtb, alog]
    out_shape = [jax.ShapeDtypeStruct((nb, d_inner), F32),
                 jax.ShapeDtypeStruct((d_inner, nb), BF16),
                 jax.ShapeDtypeStruct((nb, SSM_GROUPS * SSM_STATE), F32),
                 jax.ShapeDtypeStruct((SSM_GROUPS, SSM_STATE, nb), BF16),
                 jax.ShapeDtypeStruct((nb, LANES), F32)]
    vmem = 2 * sum(_nbytes(a.shape, a.dtype) for a in args) + 2 * sum(_nbytes(o.shape, o.dtype) for o in out_shape) \
        + 8 * _nbytes((nb, conv_dim), F32)
    return pl.pallas_call(
        _ssd_sample_prep_kernel,
        grid=(1,),
        in_specs=[_resident(a.shape) for a in args],
        out_specs=[pl.BlockSpec(o.shape, functools.partial(lambda nd, i: (0,) * nd, len(o.shape))) for o in out_shape],
        out_shape=out_shape,
        compiler_params=_cparams(("arbitrary",), vmem),
        name="ssd_sample_prep",
    )(*args)


SAMPLE_STATE_TILE = 4


def _ssd_sample_state_kernel(dec_ref, st_ref, xdt_t_ref, bdt_ref, c_t_ref, sto_ref, y_t_ref):
    i = pl.program_id(0)
    nb = bdt_ref.shape[0]
    n_heads = st_ref.shape[1]
    gw = HPG * HEAD_DIM

    @pl.when(i == 0)
    def _():
        y_t_ref[...] = jnp.zeros(y_t_ref.shape, F32)

    row = lax.broadcasted_iota(jnp.int32, (nb, SSM_STATE), 0)
    lane = lax.broadcasted_iota(jnp.int32, (gw, nb), 1)
    for j in range(SAMPLE_STATE_TILE):
        b = i * SAMPLE_STATE_TILE + j
        for g in range(SSM_GROUPS):
            b_row = jnp.where(row == b, bdt_ref[:, g * SSM_STATE:(g + 1) * SSM_STATE], 0.0).astype(BF16)
            upd = _dot(xdt_t_ref[g * gw:(g + 1) * gw, :], b_row)
            new = []
            for hg in range(HPG):
                h = g * HPG + hg
                new_h = st_ref[j, h] * dec_ref[b * n_heads + h] + upd[hg * HEAD_DIM:(hg + 1) * HEAD_DIM, :]
                sto_ref[j, h] = new_h
                new.append(new_h)
            yg = _dot(jnp.concatenate(new, axis=0).astype(BF16), c_t_ref[g])
            y_t_ref[g * gw:(g + 1) * gw, :] += jnp.where(lane == b, yg, 0.0)


def _ssd_sample_state(dec, st, xdt_t, bdt, c_t):
    nb, n_heads = st.shape[0], st.shape[1]
    d_inner = n_heads * HEAD_DIM
    bt = SAMPLE_STATE_TILE
    st_spec = pl.BlockSpec((bt, n_heads, HEAD_DIM, SSM_STATE), lambda i: (i, 0, 0, 0))
    vmem = (4 * _nbytes((bt, n_heads, HEAD_DIM, SSM_STATE), F32) + _nbytes(xdt_t.shape, BF16)
            + _nbytes(bdt.shape, F32) + _nbytes(c_t.shape, BF16) + 2 * _nbytes((d_inner, nb), F32)
            + 16 * _nbytes((HPG * HEAD_DIM, SSM_STATE), F32))
    return pl.pallas_call(
        _ssd_sample_state_kernel,
        grid=(nb // bt,),
        in_specs=[pl.BlockSpec(memory_space=pltpu.SMEM), st_spec,
                  _resident(xdt_t.shape), _resident(bdt.shape), _resident(c_t.shape)],
        out_specs=[st_spec, pl.BlockSpec((d_inner, nb), lambda i: (0, 0))],
        out_shape=[jax.ShapeDtypeStruct(st.shape, F32), jax.ShapeDtypeStruct((d_inner, nb), F32)],
        compiler_params=_cparams(("arbitrary",), vmem),
        name="ssd_sample_state",
    )(dec, st, xdt_t, bdt, c_t)


def _ssd_sample_post_kernel(y_t_ref, xs_ref, z_ref, dskip_ref, ng_ref, y_ref):
    y = y_t_ref[...].T + dskip_ref[...] * xs_ref[...]
    y_ref[...] = _ssd_gate_norm(y, z_ref[...], ng_ref[...]).astype(y_ref.dtype)


def _ssd_sample_post(y_t, xs, z, dskip, ng):
    args = [y_t, xs, z, dskip, ng]
    vmem = 2 * sum(_nbytes(a.shape, a.dtype) for a in args) + 8 * _nbytes(xs.shape, F32)
    return pl.pallas_call(
        _ssd_sample_post_kernel,
        grid=(1,),
        in_specs=[_resident(a.shape) for a in args],
        out_specs=pl.BlockSpec(xs.shape, lambda i: (0, 0)),
        out_shape=jax.ShapeDtypeStruct(xs.shape, F32),
        compiler_params=_cparams(("arbitrary",), vmem),
        name="ssd_sample_post",
    )(*args)


def _attn_sample_kernel(q_ref, qt_ref, k_ref, v_ref, c0_ref, c1_ref, c2_ref, o_ref):
    q_rows = q_ref[0] * ATTN_SCALE
    q_cols = qt_ref[0] * ATTN_SCALE
    s_new_all = jnp.sum(k_ref[0] * q_rows, axis=-1, keepdims=True)
    scores = []
    for gi, (c_ref, (_, dil)) in enumerate(zip((c0_ref, c1_ref, c2_ref), DIL_GROUPS)):
        rows = []
        for h in range(HPG):
            col = gi * HPG + h
            rows.append(jnp.sum(c_ref[0, 0, h] * q_cols[:, col:col + 1], axis=0, keepdims=True))
        s = jnp.concatenate(rows, axis=0)
        pos = lax.broadcasted_iota(jnp.int32, s.shape, 1)
        scores.append((jnp.where(pos % dil == 0, s, -jnp.inf), s_new_all[gi * HPG:(gi + 1) * HPG]))
    m = functools.reduce(jnp.maximum, [jnp.maximum(jnp.max(s, axis=1, keepdims=True), sn) for s, sn in scores])
    den = jnp.zeros((HPG, 1), F32)
    o_new = jnp.zeros((HPG, HEAD_DIM), F32)
    o_cache_t = jnp.zeros((HEAD_DIM, LANES), F32)
    lane = lax.broadcasted_iota(jnp.int32, (HEAD_DIM, LANES), 1)
    for gi, ((s, sn), c_ref) in enumerate(zip(scores, (c0_ref, c1_ref, c2_ref))):
        p = jnp.exp(s - m)
        p_n = jnp.exp(sn - m)
        den = den + jnp.sum(p, axis=1, keepdims=True) + p_n
        o_new = o_new + p_n * v_ref[0, gi * HPG:(gi + 1) * HPG, :]
        for h in range(HPG):
            contrib = jnp.sum(c_ref[0, 1, h] * p[h:h + 1, :], axis=1, keepdims=True)
            o_cache_t = jnp.where(lane == h, o_cache_t + contrib, o_cache_t)
    o_cache = jnp.concatenate([o_cache_t, jnp.zeros_like(o_cache_t)], axis=0).T
    o_ref[0] = (o_cache[:HPG, :HEAD_DIM] + o_new) / den


def _attn_sample(q, k, v, caches):
    nb, n_heads, _ = q.shape
    views, specs = [], []
    for (window, dil), c in zip(DIL_GROUPS, caches):
        assert c.shape[1] == window and window // dil == NK
        views.append(jnp.transpose(c, (0, 2, 3, 4, 1)))
        specs.append(pl.BlockSpec((1, 2, HPG, HEAD_DIM, window), lambda i: (i, 0, 0, 0, 0)))
    vec = pl.BlockSpec((1, n_heads, HEAD_DIM), lambda i: (i, 0, 0))
    vec_t = pl.BlockSpec((1, HEAD_DIM, n_heads), lambda i: (i, 0, 0))
    cache_bytes = sum(_nbytes((2, HPG, HEAD_DIM, window), F32) for window, _ in DIL_GROUPS)
    vmem = 2 * cache_bytes + 6 * _nbytes((HEAD_DIM, DIL_GROUPS[-1][0]), F32) + 8 * _nbytes((LANES, LANES), F32)
    return pl.pallas_call(
        _attn_sample_kernel,
        grid=(nb,),
        in_specs=[vec, vec_t, vec, vec] + specs,
        out_specs=pl.BlockSpec((1, HPG, HEAD_DIM), lambda i: (i, 0, 0)),
        out_shape=jax.ShapeDtypeStruct((nb, HPG, HEAD_DIM), F32),
        compiler_params=_cparams(("parallel",), vmem),
        name="attn_sample",
    )(q, jnp.swapaxes(q, 1, 2), k, v, *views)


def _row(v, width=None):
    v = v.astype(F32).reshape(1, -1)
    if width is not None and v.shape[1] < width:
        v = jnp.pad(v, ((0, 0), (0, width - v.shape[1])))
    return v


def _layer_weights(prm):
    d_model = prm['w_in'].shape[0]
    d_inner = prm['norm_ssm'].shape[0]
    n_heads = d_inner // HEAD_DIM
    conv_dim = prm['conv_w'].shape[1]
    attn_w = len(DIL_GROUPS) * HPG * HEAD_DIM
    splits = [d_inner, conv_dim, n_heads, attn_w, attn_w, attn_w, d_model, d_model]
    offs = [0]
    for n in splits:
        offs.append(offs[-1] + n)
    w_in = prm['w_in']
    seg = [w_in[:, offs[j]:offs[j + 1]].astype(BF16) for j in range(len(splits))]
    seg[2] = jnp.pad(seg[2], ((0, 0), (0, LANES - n_heads)))
    bf = lambda name: prm[name].astype(BF16)
    return dict(
        ffn1=(_row(prm['norm_ffn1']), bf('w_ffn1_gate'), bf('w_ffn1_up'), bf('w_ffn1_down'), _row(prm['norm_mix'])),
        proj_ssm=seg[0:3], proj_attn=seg[3:8],
        conv_w=prm['conv_w'].astype(F32), conv_b=_row(prm['conv_b']),
        dt_bias=_row(prm['dt_bias'], LANES), a_log=_row(prm['a_log'], LANES),
        d_skip=_row(jnp.repeat(prm['d_skip'], HEAD_DIM)), norm_ssm=_row(prm['norm_ssm']),
        out=(bf('w_o_ssm'), bf('w_o_attn'), bf('w_out'), _row(prm['norm_ffn2']), bf('w_ffn2_gate'),
             bf('w_ffn2_up'), bf('w_ffn2_down'), _row(prm['norm_ple']), bf('w_ple_gate'), bf('w_ple_proj')),
        d_inner=d_inner, conv_dim=conv_dim,
    )


def _kv_stack(k, v, gi, keep):
    b, s, _ = k.shape
    gw = HPG * HEAD_DIM
    sel = lambda t: t[:, s - keep:, gi * gw:(gi + 1) * gw].astype(F32).reshape(b, keep, HPG, HEAD_DIM)
    return jnp.stack([sel(k), sel(v)], axis=2)


def _kv_stack_wide(k, v, batch, dil, keep):
    rows = k.shape[0] // batch
    sel = lambda t: (t.reshape(batch, rows, dil * GROUP_WIDTH)[:, rows - keep // dil:, :]
                     .astype(F32).reshape(batch, keep, HPG, HEAD_DIM))
    return jnp.stack([sel(k), sel(v)], axis=2)


def _prompt_layer(x, p_emb, w, norm_final, tm):
    b, s, d = x.shape
    t = b * s
    h1, u = _ffn(x.reshape(t, d), *w['ffn1'], tm)
    z, xbc, dt = _proj(_proj_ssm_kernel, u, w['proj_ssm'], (BF16, BF16, F32), tm, "proj_ssm")
    qs, ks, vs, gs, ga = _proj_attn_wide(u, w['proj_attn'], s, tm)
    r3 = lambda a: a.reshape(b, s, a.shape[-1])
    xbc3 = r3(xbc)
    y_ssm, ssm_new = _ssd_prompt(r3(z), xbc3, r3(dt), w['conv_w'], w['conv_b'], w['dt_bias'], w['a_log'],
                                 w['d_skip'], w['norm_ssm'])
    order = sorted(range(len(DIL_GROUPS)), key=lambda gi: -DIL_GROUPS[gi][1])
    assert all(DIL_GROUPS[a][1] == MERGE_FAN * DIL_GROUPS[c][1] for a, c in zip(order, order[1:]))
    assert DIL_GROUPS[order[-1]][1] == 1
    o, lse = None, None
    for j, gi in enumerate(order):
        o, lse = _attn_prompt_group(qs[gi], ks[gi], vs[gi], b, gi, j == 0, j == len(order) - 1, o, lse)
    y = _out_stage(h1, y_ssm.reshape(t, -1), o, gs, ga, p_emb.reshape(t, -1),
                   w['out'] + (_row(norm_final),), OUT_ROW_TILE)
    kv = [_kv_stack_wide(ks[gi], vs[gi], b, dil, min(window, s)) for gi, (window, dil) in enumerate(DIL_GROUPS)]
    conv_new = xbc3[:, s - (CONV_W - 1):, :].astype(F32)
    return y.reshape(b, s, d), kv, conv_new, ssm_new


def _sample_layer(x, p_emb, w, norm_final, conv_prev, ssm_prev, caches):
    nb, s, d = x.shape
    assert s == 1
    h1, u = _ffn(x.reshape(nb, d), *w['ffn1'], nb)
    z, xbc, dt = _proj(_proj_ssm_kernel, u, w['proj_ssm'], (F32, F32, F32), nb, "proj_ssm_sample")
    q, k, v, gs, ga = _proj(functools.partial(_proj_attn_kernel, seq=1, pos0=PAST_LEN), u, w['proj_attn'],
                            (F32,) * 5, nb, "proj_attn_sample")
    xs, xdt_t, bdt, c_t, dec = _ssd_sample_prep(xbc, conv_prev.reshape(nb, -1), dt, w['conv_w'], w['conv_b'],
                                                w['dt_bias'], w['a_log'], w['d_inner'])
    dec_flat = dec[:, :ssm_prev.shape[1]].reshape(-1)
    ssm_new, y_t = _ssd_sample_state(dec_flat, ssm_prev, xdt_t, bdt, c_t)
    y_ssm = _ssd_sample_post(y_t, xs, z, w['d_skip'], w['norm_ssm'])
    by_head = lambda a: a.reshape(nb, -1, HEAD_DIM)
    attn = _attn_sample(by_head(q), by_head(k), by_head(v), caches)
    y = _out_stage(h1, y_ssm, attn.reshape(nb, -1), gs, ga, p_emb.reshape(nb, -1),
                   w['out'] + (_row(norm_final),), nb)
    r3 = lambda a: a.reshape(nb, 1, a.shape[-1])
    kv = [_kv_stack(r3(k), r3(v), gi, 1) for gi in range(len(DIL_GROUPS))]
    conv_new = jnp.concatenate([conv_prev[:, 1:], xbc[:, None, :]], axis=1)
    return y.reshape(nb, 1, d), kv, conv_new, ssm_new


PROMPT_ROW_TILE = 512
OUT_ROW_TILE = 512


def kernel(x_prompt, x_sample, cache_kv_w128, cache_kv_w512, cache_kv_w2048, state_conv, state_ssm, p_prompt, p_sample, norm_ffn1, w_ffn1_gate, w_ffn1_up, w_ffn1_down, norm_mix, w_in, conv_w, conv_b, dt_bias, a_log, d_skip, norm_ssm, w_o_ssm, w_o_attn, w_out, norm_ffn2, w_ffn2_gate, w_ffn2_up, w_ffn2_down, norm_ple, w_ple_gate, w_ple_proj, norm_final):
    depth = w_in.shape[0]
    assert depth == 1
    layer_params = dict(
        norm_ffn1=norm_ffn1, w_ffn1_gate=w_ffn1_gate, w_ffn1_up=w_ffn1_up, w_ffn1_down=w_ffn1_down,
        norm_mix=norm_mix, w_in=w_in, conv_w=conv_w, conv_b=conv_b, dt_bias=dt_bias, a_log=a_log,
        d_skip=d_skip, norm_ssm=norm_ssm, w_o_ssm=w_o_ssm, w_o_attn=w_o_attn, w_out=w_out,
        norm_ffn2=norm_ffn2, w_ffn2_gate=w_ffn2_gate, w_ffn2_up=w_ffn2_up, w_ffn2_down=w_ffn2_down,
        norm_ple=norm_ple, w_ple_gate=w_ple_gate, w_ple_proj=w_ple_proj)
    i = 0
    w = _layer_weights({name: val[i] for name, val in layer_params.items()})
    yp, kvp, convp, ssmp = _prompt_layer(x_prompt, p_prompt[i], w, norm_final, PROMPT_ROW_TILE)
    ys, kvs, convs, ssms = _sample_layer(x_sample, p_sample[i], w, norm_final, state_conv[i], state_ssm[i],
                                         (cache_kv_w128[i], cache_kv_w512[i], cache_kv_w2048[i]))
    st = lambda a: a[None]
    return (yp, ys, st(kvp[0]), st(kvp[1]), st(kvp[2]), st(convp), st(ssmp),
            st(kvs[0]), st(kvs[1]), st(kvs[2]), st(convs), st(ssms))
```

```python
import functools
import math

import jax
import jax.numpy as jnp
from jax import lax
from jax.experimental import pallas as pl
from jax.experimental.pallas import tpu as pltpu

F32 = jnp.float32
BF16 = jnp.bfloat16

EPS = 1e-6
ROPE_THETA = 10000.0
PAST_LEN = 8192
HEAD_DIM = 64
HPG = 8
SSM_GROUPS = 4
SSM_STATE = 128
CONV_W = 4
DIL_GROUPS = ((128, 1), (512, 4), (2048, 16))
ATTN_SCALE = HEAD_DIM ** -0.5
NK = 128

LANES = 128
SUBLANES = 8
VMEM_LIMIT_CAP = 56 * 1024 * 1024


def _cparams(sem, vmem_bytes):
    return pltpu.CompilerParams(dimension_semantics=sem,
                                vmem_limit_bytes=int(min(VMEM_LIMIT_CAP, vmem_bytes)))


def _resident(shape):
    nd = len(shape)
    return pl.BlockSpec(shape, lambda *_: (0,) * nd, pipeline_mode=pl.Buffered(1))


def _nbytes(shape, dtype):
    return math.prod(shape) * jnp.dtype(dtype).itemsize


def _rms(x, gain):
    ms = jnp.mean(x * x, axis=-1, keepdims=True)
    return x * lax.rsqrt(ms + EPS) * gain


def _sigmoid(x):
    return 1.0 / (1.0 + jnp.exp(-x))


def _silu(x):
    half = 0.5 * x
    return half + half * jnp.tanh(half)


def _softplus(x):
    return jnp.maximum(x, 0.0) + jnp.log1p(jnp.exp(-jnp.abs(x)))


def _dot(a, b):
    return jnp.dot(a, b, preferred_element_type=F32)


def _dot_nt(a, b):
    return lax.dot_general(a, b, (((1,), (1,)), ((), ())), preferred_element_type=F32)


def _split3(x):
    hi = x.astype(BF16)
    r1 = x - hi.astype(F32)
    mid = r1.astype(BF16)
    lo = (r1 - mid.astype(F32)).astype(BF16)
    return hi, mid, lo


def _dot_sel_l(sel, x):
    hi, mid, lo = _split3(x)
    return _dot(sel, hi) + _dot(sel, mid) + _dot(sel, lo)


def _dot_sel_r(x, sel):
    hi, mid, lo = _split3(x)
    return _dot(hi, sel) + _dot(mid, sel) + _dot(lo, sel)


def _head_expand(n_heads_padded, n_heads):
    r = lax.broadcasted_iota(jnp.int32, (n_heads_padded, n_heads * HEAD_DIM), 0)
    c = lax.broadcasted_iota(jnp.int32, (n_heads_padded, n_heads * HEAD_DIM), 1)
    return (c // HEAD_DIM == r).astype(BF16)


FF_CHUNK = 256


def _swiglu_half(x, gain, wg_ref, wu_ref, wd_ref):
    u = _rms(x, gain).astype(BF16)
    d_ff = wg_ref.shape[1]
    acc = jnp.zeros(x.shape, F32)
    for c in range(d_ff // FF_CHUNK):
        sl = slice(c * FF_CHUNK, (c + 1) * FF_CHUNK)
        g = _dot(u, wg_ref[:, sl])
        up = _dot(u, wu_ref[:, sl])
        acc = acc + _dot((_silu(g) * up).astype(BF16), wd_ref[sl, :])
    return x + 0.5 * acc


def _ffn_kernel(x_ref, n1_ref, wg_ref, wu_ref, wd_ref, n2_ref, h_ref, u_ref):
    h = _swiglu_half(x_ref[...], n1_ref[...], wg_ref, wu_ref, wd_ref)
    h_ref[...] = h
    u_ref[...] = _rms(h, n2_ref[...]).astype(u_ref.dtype)


def _ffn(x, n1, wg, wu, wd, n2, tm):
    t, d = x.shape
    d_ff = wg.shape[1]
    vmem = (3 * _nbytes((d, d_ff), BF16) + 2 * 2 * _nbytes((tm, d), F32) + 2 * _nbytes((tm, d), BF16)
            + 6 * _nbytes((tm, d), F32) + 4 * _nbytes((tm, FF_CHUNK), F32))
    return pl.pallas_call(
        _ffn_kernel,
        grid=(t // tm,),
        in_specs=[pl.BlockSpec((tm, d), lambda i: (i, 0)),
                  _resident((1, d)), _resident((d, d_ff)), _resident((d, d_ff)), _resident((d_ff, d)),
                  _resident((1, d))],
        out_specs=[pl.BlockSpec((tm, d), lambda i: (i, 0)), pl.BlockSpec((tm, d), lambda i: (i, 0))],
        out_shape=[jax.ShapeDtypeStruct((t, d), F32), jax.ShapeDtypeStruct((t, d), BF16)],
        compiler_params=_cparams(("parallel",), vmem),
        name="ffn_half_step",
    )(x, n1, wg, wu, wd, n2)


def _rope_tables(tm, seq, pos0):
    row = lax.broadcasted_iota(jnp.int32, (tm, LANES), 0) + pl.program_id(0) * tm
    pos = (row % seq + pos0).astype(F32)
    lane = lax.broadcasted_iota(jnp.int32, (tm, LANES), 1)
    half = HEAD_DIM // 2
    j = (lane % half).astype(F32)
    inv_freq = jnp.exp(j * (-math.log(ROPE_THETA) / half))
    ang = pos * inv_freq
    first = (lane % HEAD_DIM) < half
    return jnp.cos(ang), jnp.where(first, -jnp.sin(ang), jnp.sin(ang)), first


def _rotary(x, cos, sin_signed, first):
    outs = []
    for c in range(x.shape[1] // LANES):
        xb = x[:, c * LANES:(c + 1) * LANES]
        other = jnp.where(first, pltpu.roll(xb, LANES - HEAD_DIM // 2, 1), pltpu.roll(xb, HEAD_DIM // 2, 1))
        outs.append(xb * cos + other * sin_signed)
    return jnp.concatenate(outs, axis=1)


def _proj_ssm_kernel(u_ref, wz_ref, wx_ref, wdt_ref, z_ref, xbc_ref, dt_ref):
    u = u_ref[...]
    z_ref[...] = _dot(u, wz_ref[...]).astype(z_ref.dtype)
    xbc_ref[...] = _dot(u, wx_ref[...]).astype(xbc_ref.dtype)
    dt_ref[...] = _dot(u, wdt_ref[...])


def _proj_attn_kernel(u_ref, wq_ref, wk_ref, wv_ref, wgs_ref, wga_ref,
                      q_ref, k_ref, v_ref, gs_ref, ga_ref, *, seq, pos0):
    u = u_ref[...]
    cos, sin_signed, first = _rope_tables(u.shape[0], seq, pos0)
    q_ref[...] = _rotary(_dot(u, wq_ref[...]), cos, sin_signed, first).astype(q_ref.dtype)
    k_ref[...] = _rotary(_dot(u, wk_ref[...]), cos, sin_signed, first).astype(k_ref.dtype)
    v_ref[...] = _dot(u, wv_ref[...]).astype(v_ref.dtype)
    gs_ref[...] = _sigmoid(_dot(u, wgs_ref[...])).astype(gs_ref.dtype)
    ga_ref[...] = _sigmoid(_dot(u, wga_ref[...])).astype(ga_ref.dtype)


GROUP_WIDTH = HPG * HEAD_DIM
_STAGED_GROUPS = [gi for gi, (_, dil) in enumerate(DIL_GROUPS) if dil > 1]


def _store_wide(stage, val, out_ref, dil):
    if dil == 1:
        out_ref[...] = val.astype(out_ref.dtype)
        return
    rows = val.shape[0] // dil
    for j in range(GROUP_WIDTH // LANES):
        stage[j] = val[:, j * LANES:(j + 1) * LANES]
    for r in range(dil):
        for j in range(GROUP_WIDTH // LANES):
            lo = r * GROUP_WIDTH + j * LANES
            out_ref[:, lo:lo + LANES] = stage[j, pl.ds(r, rows, stride=dil), :].astype(out_ref.dtype)


def _rope_table_kernel(cos_ref, sin_ref, *, seq):
    cos, sin_signed, _ = _rope_tables(cos_ref.shape[0], seq, 0)
    cos_ref[...] = cos
    sin_ref[...] = sin_signed


def _rope_table(seq, tm):
    spec = pl.BlockSpec((tm, LANES), lambda i: (i, 0))
    shape = jax.ShapeDtypeStruct((seq, LANES), F32)
    return pl.pallas_call(
        functools.partial(_rope_table_kernel, seq=seq),
        grid=(seq // tm,), in_specs=[], out_specs=[spec, spec], out_shape=[shape, shape],
        compiler_params=_cparams(("parallel",), 16 * _nbytes((tm, LANES), F32)),
        name="rope_table",
    )()


def _proj_attn_wide_kernel(u_ref, cos_ref, sin_ref, wq_ref, wk_ref, wv_ref, wgs_ref, wga_ref, *refs):
    n_grp = len(DIL_GROUPS)
    q_refs, k_refs, v_refs = refs[0:n_grp], refs[n_grp:2 * n_grp], refs[2 * n_grp:3 * n_grp]
    gs_ref, ga_ref, stage = refs[3 * n_grp:]
    u = u_ref[...]
    cos, sin_signed = cos_ref[...], sin_ref[...]
    first = lax.broadcasted_iota(jnp.int32, cos.shape, 1) % HEAD_DIM < HEAD_DIM // 2
    for ti, (w_ref, out_refs, rope) in enumerate(((wq_ref, q_refs, True), (wk_ref, k_refs, True),
                                                 (wv_ref, v_refs, False))):
        for gi, (_, dil) in enumerate(DIL_GROUPS):
            res = _dot(u, w_ref[:, gi * GROUP_WIDTH:(gi + 1) * GROUP_WIDTH])
            if rope:
                res = _rotary(res, cos, sin_signed, first)
            slot = None if dil == 1 else stage.at[ti * len(_STAGED_GROUPS) + _STAGED_GROUPS.index(gi)]
            _store_wide(slot, res, out_refs[gi], dil)
    gs_ref[...] = _sigmoid(_dot(u, wgs_ref[...])).astype(gs_ref.dtype)
    ga_ref[...] = _sigmoid(_dot(u, wga_ref[...])).astype(ga_ref.dtype)


def _proj_attn_wide(u, weights, seq, tm):
    t, d = u.shape
    d_model = weights[3].shape[1]
    row = lambda i: (i, 0)
    qkv_specs = [pl.BlockSpec((tm // dil, dil * GROUP_WIDTH), row) for _, dil in DIL_GROUPS] * 3
    qkv_shapes = [jax.ShapeDtypeStruct((t // dil, dil * GROUP_WIDTH), BF16) for _, dil in DIL_GROUPS] * 3
    n_stage = 3 * len(_STAGED_GROUPS)
    vmem = (sum(_nbytes(w.shape, BF16) for w in weights) + 2 * _nbytes((tm, d), BF16)
            + 2 * _nbytes((tm, 3 * len(DIL_GROUPS) * GROUP_WIDTH + 2 * d_model), BF16)
            + 4 * _nbytes((tm, LANES), F32)
            + n_stage * _nbytes((tm, GROUP_WIDTH), F32) + 6 * _nbytes((tm, weights[0].shape[1]), F32))
    cos, sin_signed = _rope_table(seq, tm)
    tiles_per_seq = seq // tm
    table = pl.BlockSpec((tm, LANES), lambda i: (i % tiles_per_seq, 0))
    outs = pl.pallas_call(
        _proj_attn_wide_kernel,
        grid=(t // tm,),
        in_specs=[pl.BlockSpec((tm, d), row), table, table] + [_resident(w.shape) for w in weights],
        out_specs=qkv_specs + [pl.BlockSpec((tm, d_model), row)] * 2,
        out_shape=qkv_shapes + [jax.ShapeDtypeStruct((t, d_model), BF16)] * 2,
        scratch_shapes=[pltpu.VMEM((n_stage, GROUP_WIDTH // LANES, tm, LANES), F32)],
        compiler_params=_cparams(("parallel",), vmem),
        name="proj_attn",
    )(u, cos, sin_signed, *weights)
    n_grp = len(DIL_GROUPS)
    return outs[0:n_grp], outs[n_grp:2 * n_grp], outs[2 * n_grp:3 * n_grp], outs[3 * n_grp], outs[3 * n_grp + 1]


def _proj(kernel, u, weights, out_dtypes, tm, name):
    t, d = u.shape
    row = lambda i: (i, 0)
    widths = [w.shape[1] for w in weights]
    vmem = (sum(_nbytes(w.shape, BF16) for w in weights) + 2 * _nbytes((tm, d), BF16)
            + sum(2 * _nbytes((tm, n), dt) for n, dt in zip(widths, out_dtypes))
            + 4 * _nbytes((tm, max(widths)), F32))
    return pl.pallas_call(
        kernel,
        grid=(t // tm,),
        in_specs=[pl.BlockSpec((tm, d), row)] + [_resident(w.shape) for w in weights],
        out_specs=[pl.BlockSpec((tm, n), row) for n in widths],
        out_shape=[jax.ShapeDtypeStruct((t, n), dt) for n, dt in zip(widths, out_dtypes)],
        compiler_params=_cparams(("parallel",), vmem),
        name=name,
    )(u, *weights)


SSD_CHUNK = 128
SSD_CHUNKS_PER_STEP = 4


def _ssd_gate_norm(y, z, gain):
    return _rms(y * _silu(z), gain)


def _ssd_chunk(x_prev, x_cur, z, dt_raw, states, cw_ref, cb_ref, dtb_ref, alog_ref, dskip_ref, ng_ref,
               shift_ref, expand_ref):
    L = SSD_CHUNK
    d_inner = z.shape[-1]
    gw = HPG * HEAD_DIM
    gn = SSM_GROUPS * SSM_STATE

    both = jnp.concatenate([x_prev, x_cur], axis=0)
    conv = cb_ref[...] + x_cur.astype(F32) * cw_ref[CONV_W - 1:CONV_W, :]
    shifted = _dot(shift_ref[...], both)
    for back in range(1, CONV_W):
        tap = CONV_W - 1 - back
        conv = conv + shifted[(back - 1) * L:back * L, :] * cw_ref[tap:tap + 1, :]
    xa = _silu(conv)
    xs = xa[:, :d_inner]
    bm = xa[:, d_inner:d_inner + gn]
    cm = xa[:, d_inner + gn:]

    dt = _softplus(dt_raw + dtb_ref[...])
    a = -jnp.exp(alog_ref[...])
    row = lax.broadcasted_iota(jnp.int32, (L, L), 0)
    col = lax.broadcasted_iota(jnp.int32, (L, L), 1)
    causal = row >= col
    acs = _dot_sel_l(causal.astype(BF16), dt * a)
    acs_t = acs.T
    dt_t = dt.T
    acs_last = acs[L - 1:L, :]

    expand = expand_ref[...]
    per_head = jnp.concatenate([jnp.exp(acs), jnp.exp(acs_last - acs) * dt], axis=0)
    per_lane = _dot(per_head.astype(BF16), expand)
    e_acs, w_end = per_lane[:L], per_lane[L:]
    dec_state = _dot_sel_r(jnp.broadcast_to(jnp.exp(acs_last), (SUBLANES, LANES)), expand)[0:1, :]

    xs_b = xs.astype(BF16)
    xw = (xs * w_end).astype(BF16)
    lane_first = lax.broadcasted_iota(jnp.int32, (L, LANES), 1) < HEAD_DIM
    y_groups, new_states = [], []
    for g in range(SSM_GROUPS):
        gsl = slice(g * gw, (g + 1) * gw)
        b_g = bm[:, g * SSM_STATE:(g + 1) * SSM_STATE]
        c_g = cm[:, g * SSM_STATE:(g + 1) * SSM_STATE].astype(BF16)
        cb = _dot_nt(c_g, b_g.astype(BF16))
        st_g = states[g]
        y_inter = _dot(c_g, st_g.astype(BF16))
        pairs = []
        for pr in range(HPG // 2):
            w_pair = []
            for sub in range(2):
                h = g * HPG + pr * 2 + sub
                seg = acs[:, h:h + 1] - acs_t[h:h + 1, :]
                w = cb * jnp.exp(jnp.where(causal, seg, -jnp.inf)) * dt_t[h:h + 1, :]
                w_pair.append(w.astype(BF16))
            lo = g * gw + pr * LANES
            res = _dot(jnp.concatenate(w_pair, axis=0), xs_b[:, lo:lo + LANES])
            pairs.append(jnp.where(lane_first, res[:L], res[L:]))
        y_groups.append(jnp.concatenate(pairs, axis=1) + y_inter * e_acs[:, gsl])
        new_states.append(dec_state[:, gsl] * st_g + _dot(b_g.T.astype(BF16), xw[:, gsl]))
    y = jnp.concatenate(y_groups, axis=1) + dskip_ref[...] * xs
    return _ssd_gate_norm(y, z.astype(F32), ng_ref[...]), new_states


def _ssd_prompt_kernel(z_ref, xprev_ref, xbc_ref, dt_ref, cw_ref, cb_ref, dtb_ref, alog_ref, dskip_ref, ng_ref,
                       shift_ref, expand_ref, y_ref, st_ref, st_t):
    L = SSD_CHUNK
    c = pl.program_id(1)

    @pl.when(c == 0)
    def _():
        st_t[...] = jnp.zeros(st_t.shape, F32)

    states = [st_t[g] for g in range(SSM_GROUPS)]
    x_prev = jnp.where(c > 0, xprev_ref[0], jnp.zeros(xprev_ref.shape[1:], xprev_ref.dtype))
    for j in range(xbc_ref.shape[1] // L):
        rows = slice(j * L, (j + 1) * L)
        x_cur = xbc_ref[0, rows, :]
        y, states = _ssd_chunk(x_prev, x_cur, z_ref[0, rows, :], dt_ref[0, rows, :], states,
                               cw_ref, cb_ref, dtb_ref, alog_ref, dskip_ref, ng_ref, shift_ref, expand_ref)
        y_ref[0, rows, :] = y.astype(y_ref.dtype)
        x_prev = x_cur
    for g in range(SSM_GROUPS):
        st_t[g] = states[g]

    @pl.when(c == pl.num_programs(1) - 1)
    def _():
        for g in range(SSM_GROUPS):
            st_ref[0, g * HPG:(g + 1) * HPG] = states[g].T.reshape(HPG, HEAD_DIM, SSM_STATE)


def _ssd_prompt(z, xbc, dt, cw, cb, dtb, alog, dskip, ng):
    b, s, d_inner = z.shape
    conv_dim = xbc.shape[-1]
    n_heads = d_inner // HEAD_DIM
    L = SSD_CHUNK
    per_step = math.gcd(s // L, SSD_CHUNKS_PER_STEP)
    blk = lambda n: pl.BlockSpec((1, per_step * L, n), lambda i, c: (i, c, 0))
    r = jnp.arange(L)[:, None]
    cidx = jnp.arange(2 * L)[None, :]
    shift = jnp.concatenate([(cidx == r + L - k) for k in range(1, CONV_W)], axis=0).astype(BF16)
    heads = jnp.arange(LANES)[:, None]
    expand = (jnp.arange(d_inner)[None, :] // HEAD_DIM == heads).astype(BF16)
    prev_blk = pl.BlockSpec((1, L, conv_dim), lambda i, c: (i, jnp.maximum(c * per_step - 1, 0), 0))
    vmem = (2 * per_step * (_nbytes((L, d_inner), z.dtype) * 2 + _nbytes((L, conv_dim), xbc.dtype)
                            + _nbytes((L, LANES), F32))
            + 2 * _nbytes((L, conv_dim), xbc.dtype)
            + 2 * _nbytes((n_heads, HEAD_DIM, SSM_STATE), F32)
            + _nbytes((SSM_GROUPS, SSM_STATE, HPG * HEAD_DIM), F32)
            + _nbytes(shift.shape, BF16) + _nbytes(expand.shape, BF16)
            + 16 * per_step * _nbytes((L, conv_dim), F32))
    return pl.pallas_call(
        _ssd_prompt_kernel,
        grid=(b, s // (per_step * L)),
        in_specs=[blk(d_inner), prev_blk, blk(conv_dim), blk(LANES),
                  _resident(cw.shape), _resident(cb.shape), _resident(dtb.shape), _resident(alog.shape),
                  _resident(dskip.shape), _resident(ng.shape), _resident(shift.shape), _resident(expand.shape)],
        out_specs=[blk(d_inner),
                   pl.BlockSpec((1, n_heads, HEAD_DIM, SSM_STATE), lambda i, c: (i, 0, 0, 0))],
        out_shape=[jax.ShapeDtypeStruct((b, s, d_inner), BF16),
                   jax.ShapeDtypeStruct((b, n_heads, HEAD_DIM, SSM_STATE), F32)],
        scratch_shapes=[pltpu.VMEM((SSM_GROUPS, SSM_STATE, HPG * HEAD_DIM), F32)],
        compiler_params=_cparams(("parallel", "arbitrary"), vmem),
        name="ssd_prompt_scan",
    )(z, xbc, xbc, dt, cw, cb, dtb, alog, dskip, ng, shift, expand)


MERGE_FAN = 4
ATTN_BLOCKS_PER_STEP = 4


def _attn_prompt_kernel(*refs, first, last):
    q_ref, kp_ref, kc_ref, vp_ref, vc_ref = refs[:5]
    refs = refs[5:]
    if not first:
        oin_refs, lin_refs = refs[:MERGE_FAN], refs[MERGE_FAN:2 * MERGE_FAN]
        refs = refs[2 * MERGE_FAN:]
    out_refs = refs
    o_ref = out_refs[0]
    n = pl.program_id(2)
    n_blocks = q_ref.shape[0] // NK
    qi = lax.broadcasted_iota(jnp.int32, (NK, 2 * NK), 0)
    kj = lax.broadcasted_iota(jnp.int32, (NK, 2 * NK), 1)
    dist = qi + NK - kj
    band = (dist >= 0) & (dist <= NK)
    lane = lax.broadcasted_iota(jnp.int32, (NK, LANES), 1)
    lane_first = lane < HEAD_DIM
    if not first:
        rows = NK // MERGE_FAN
        dst = lax.broadcasted_iota(jnp.int32, (NK, NK), 0)
        src = lax.broadcasted_iota(jnp.int32, (NK, NK), 1)
        perm = (src == (dst % MERGE_FAN) * rows + dst // MERGE_FAN).astype(BF16)
        expand = _head_expand(LANES, HPG)
    for j in range(n_blocks):
        own = slice(j * NK, (j + 1) * NK)
        q = q_ref[own, :] * ATTN_SCALE
        if j == 0:
            k_before, v_before = kp_ref[...], vp_ref[...]
            mask = band & ((kj >= NK) | (n > 0))
        else:
            before = slice((j - 1) * NK, j * NK)
            k_before, v_before = kc_ref[before, :], vc_ref[before, :]
            mask = band
        kk = jnp.concatenate([k_before, kc_ref[own, :]], axis=0)
        vv = jnp.concatenate([v_before, vc_ref[own, :]], axis=0)
        lse_out = jnp.zeros((NK, LANES), F32)
        o_parts = []
        for pr in range(HPG // 2):
            sl = slice(pr * LANES, (pr + 1) * LANES)
            q_pair, k_pair, v_pair = q[:, sl], kk[:, sl], vv[:, sl]
            res = []
            for sub in range(2):
                h = pr * 2 + sub
                qm = jnp.where(lane_first if sub == 0 else ~lane_first, q_pair, jnp.zeros_like(q_pair))
                s = jnp.where(mask, _dot_nt(qm, k_pair), -jnp.inf)
                m = jnp.max(s, axis=-1, keepdims=True)
                p = jnp.exp(s - m)
                l = jnp.sum(p, axis=-1, keepdims=True)
                res.append(_dot(p.astype(BF16), v_pair) / l)
                lse_out = jnp.where(lane == h, m + jnp.log(l), lse_out)
            o_parts.append(jnp.where(lane_first, res[0], res[1]))
        o = jnp.concatenate(o_parts, axis=1)
        if not first:
            part = slice(j * rows, (j + 1) * rows)
            o_prev = _dot(perm, jnp.concatenate([r[part, :] for r in oin_refs], axis=0))
            lse_prev = _dot_sel_l(perm, jnp.concatenate([r[part, :] for r in lin_refs], axis=0))
            lse_new = jnp.maximum(lse_out, lse_prev) + jnp.log1p(jnp.exp(-jnp.abs(lse_out - lse_prev)))
            w_prev = _dot_sel_r(jnp.exp(lse_prev - lse_new), expand)
            w_cur = _dot_sel_r(jnp.exp(lse_out - lse_new), expand)
            o = o_prev * w_prev + o * w_cur
            lse_out = lse_new
        o_ref[own, :] = o.astype(o_ref.dtype)
        if not last:
            out_refs[1][own, :] = lse_out


def _attn_prompt_group(q, k, v, batch, gi, first, last, o_in, lse_in):
    window, dil = DIL_GROUPS[gi]
    td = q.shape[0]
    sd = td // batch
    assert window // dil == NK and sd % NK == 0 and q.shape[1] == dil * GROUP_WIDTH
    nb = sd // NK
    g = math.gcd(nb, ATTN_BLOCKS_PER_STEP)
    ns = nb // g
    blk = lambda width: pl.BlockSpec((g * NK, width), lambda i, r, n: (i * ns + n, r))
    prev = pl.BlockSpec((NK, GROUP_WIDTH), lambda i, r, n: (i * nb + jnp.maximum(n * g - 1, 0), r))
    args, in_specs = [q, k, k, v, v], [blk(GROUP_WIDTH), prev, blk(GROUP_WIDTH), prev, blk(GROUP_WIDTH)]
    if not first:
        assert o_in.shape == (td // MERGE_FAN, MERGE_FAN * dil * GROUP_WIDTH)
        piece = lambda width, c: pl.BlockSpec((g * NK // MERGE_FAN, width),
                                              lambda i, r, n: (i * ns + n, dil * c + r))
        args += [o_in] * MERGE_FAN + [lse_in] * MERGE_FAN
        in_specs += [piece(GROUP_WIDTH, c) for c in range(MERGE_FAN)] + [piece(LANES, c) for c in range(MERGE_FAN)]
    out_specs = [blk(GROUP_WIDTH)]
    out_shape = [jax.ShapeDtypeStruct((td, dil * GROUP_WIDTH), BF16)]
    if not last:
        out_specs.append(blk(LANES))
        out_shape.append(jax.ShapeDtypeStruct((td, dil * LANES), F32))
    vmem = (2 * (6 * g + 2) * _nbytes((NK, GROUP_WIDTH), BF16) + 6 * g * _nbytes((NK, LANES), F32)
            + 24 * _nbytes((NK, 2 * NK), F32) + 6 * g * _nbytes((NK, GROUP_WIDTH), F32))
    outs = pl.pallas_call(
        functools.partial(_attn_prompt_kernel, first=first, last=last),
        grid=(batch, dil, ns),
        in_specs=in_specs, out_specs=out_specs, out_shape=out_shape,
        compiler_params=_cparams(("parallel", "parallel", "parallel"), vmem),
        name=f"attn_prompt_w{window}",
    )(*args)
    return outs[0], (None if last else outs[1])


def _out_kernel(h_ref, y_ref, at_ref, gs_ref, ga_ref, p_ref,
                wos_ref, woa_ref, wo_ref, n2_ref, wg_ref, wu_ref, wd_ref, np_ref, wpg_ref, wpp_ref, nf_ref,
                o_ref):
    merged = (gs_ref[...].astype(F32) * _dot(y_ref[...].astype(BF16), wos_ref[...])
              + ga_ref[...].astype(F32) * _dot(at_ref[...].astype(BF16), woa_ref[...]))
    h = h_ref[...] + _dot(merged.astype(BF16), wo_ref[...])
    h = _swiglu_half(h, n2_ref[...], wg_ref, wu_ref, wd_ref)
    gate = _sigmoid(_dot(_rms(h, np_ref[...]).astype(BF16), wpg_ref[...]))
    h = h + gate * _dot(p_ref[...].astype(BF16), wpp_ref[...])
    o_ref[...] = _rms(h, nf_ref[...])


def _out_stage(h, y, attn, gs, ga, p, weights, tm):
    t, d = h.shape
    row = lambda i: (i, 0)
    acts = [h, y, attn, gs, ga, p]
    vmem = (sum(_nbytes(w.shape, w.dtype) for w in weights)
            + sum(2 * _nbytes((tm, a.shape[1]), a.dtype) for a in acts) + 2 * _nbytes((tm, d), F32)
            + 8 * _nbytes((tm, d), F32) + 4 * _nbytes((tm, FF_CHUNK), F32))
    return pl.pallas_call(
        _out_kernel,
        grid=(t // tm,),
        in_specs=[pl.BlockSpec((tm, a.shape[1]), row) for a in acts] + [_resident(w.shape) for w in weights],
        out_specs=pl.BlockSpec((tm, d), row),
        out_shape=jax.ShapeDtypeStruct((t, d), F32),
        compiler_params=_cparams(("parallel",), vmem),
        name="output_stage",
    )(*acts, *weights)


def _ssd_sample_prep_kernel(xn_ref, sc_ref, dtr_ref, cw_ref, cb_ref, dtb_ref, alog_ref,
                            xs_ref, xdt_t_ref, bdt_ref, c_t_ref, dec_ref):
    conv_dim = xn_ref.shape[1]
    d_inner = xs_ref.shape[1]
    gn = SSM_GROUPS * SSM_STATE
    conv = cb_ref[...] + xn_ref[...] * cw_ref[CONV_W - 1:CONV_W, :]
    for tap in range(CONV_W - 1):
        conv = conv + sc_ref[:, tap * conv_dim:(tap + 1) * conv_dim] * cw_ref[tap:tap + 1, :]
    xa = _silu(conv)
    xs = xa[:, :d_inner]
    xs_ref[...] = xs
    dt = _softplus(dtr_ref[...] + dtb_ref[...])
    dec_ref[...] = jnp.exp(dt * -jnp.exp(alog_ref[...]))
    n_heads = d_inner // HEAD_DIM
    dt_x = _dot_sel_r(dt, _head_expand(LANES, n_heads))
    xdt_t_ref[...] = (xs * dt_x).T.astype(xdt_t_ref.dtype)
    bdt_ref[...] = xa[:, d_inner:d_inner + gn]
    for g in range(SSM_GROUPS):
        lo = d_inner + gn + g * SSM_STATE
        c_t_ref[g] = xa[:, lo:lo + SSM_STATE].T.astype(c_t_ref.dtype)


def _ssd_sample_prep(xn, sc, dtr, cw, cb, dtb, alog, d_inner):
    nb, conv_dim = xn.shape
    args = [xn, sc, dtr, cw, cb, dtb, alog]
    out_shape = [jax.ShapeDtypeStruct((nb, d_inner), F32),
                 jax.ShapeDtypeStruct((d_inner, nb), BF16),
                 jax.ShapeDtypeStruct((nb, SSM_GROUPS * SSM_STATE), F32),
                 jax.ShapeDtypeStruct((SSM_GROUPS, SSM_STATE, nb), BF16),
                 jax.ShapeDtypeStruct((nb, LANES), F32)]
    vmem = 2 * sum(_nbytes(a.shape, a.dtype) for a in args) + 2 * sum(_nbytes(o.shape, o.dtype) for o in out_shape) \
        + 8 * _nbytes((nb, conv_dim), F32)
    return pl.pallas_call(
        _ssd_sample_prep_kernel,
        grid=(1,),
        in_specs=[_resident(a.shape) for a in args],
        out_specs=[pl.BlockSpec(o.shape, functools.partial(lambda nd, i: (0,) * nd, len(o.shape))) for o in out_shape],
        out_shape=out_shape,
        compiler_params=_cparams(("arbitrary",), vmem),
        name="ssd_sample_prep",
    )(*args)


SAMPLE_STATE_TILE = 4


def _ssd_sample_state_kernel(dec_ref, st_ref, xdt_t_ref, bdt_ref, c_t_ref, sto_ref, y_t_ref):
    i = pl.program_id(0)
    nb = bdt_ref.shape[0]
    n_heads = st_ref.shape[1]
    gw = HPG * HEAD_DIM

    @pl.when(i == 0)
    def _():
        y_t_ref[...] = jnp.zeros(y_t_ref.shape, F32)

    row = lax.broadcasted_iota(jnp.int32, (nb, SSM_STATE), 0)
    lane = lax.broadcasted_iota(jnp.int32, (gw, nb), 1)
    for j in range(SAMPLE_STATE_TILE):
        b = i * SAMPLE_STATE_TILE + j
        for g in range(SSM_GROUPS):
            b_row = jnp.where(row == b, bdt_ref[:, g * SSM_STATE:(g + 1) * SSM_STATE], 0.0).astype(BF16)
            upd = _dot(xdt_t_ref[g * gw:(g + 1) * gw, :], b_row)
            new = []
            for hg in range(HPG):
                h = g * HPG + hg
                new_h = st_ref[j, h] * dec_ref[b * n_heads + h] + upd[hg * HEAD_DIM:(hg + 1) * HEAD_DIM, :]
                sto_ref[j, h] = new_h
                new.append(new_h)
            yg = _dot(jnp.concatenate(new, axis=0).astype(BF16), c_t_ref[g])
            y_t_ref[g * gw:(g + 1) * gw, :] += jnp.where(lane == b, yg, 0.0)


def _ssd_sample_state(dec, st, xdt_t, bdt, c_t):
    nb, n_heads = st.shape[0], st.shape[1]
    d_inner = n_heads * HEAD_DIM
    bt = SAMPLE_STATE_TILE
    st_spec = pl.BlockSpec((bt, n_heads, HEAD_DIM, SSM_STATE), lambda i: (i, 0, 0, 0))
    vmem = (4 * _nbytes((bt, n_heads, HEAD_DIM, SSM_STATE), F32) + _nbytes(xdt_t.shape, BF16)
            + _nbytes(bdt.shape, F32) + _nbytes(c_t.shape, BF16) + 2 * _nbytes((d_inner, nb), F32)
            + 16 * _nbytes((HPG * HEAD_DIM, SSM_STATE), F32))
    return pl.pallas_call(
        _ssd_sample_state_kernel,
        grid=(nb // bt,),
        in_specs=[pl.BlockSpec(memory_space=pltpu.SMEM), st_spec,
                  _resident(xdt_t.shape), _resident(bdt.shape), _resident(c_t.shape)],
        out_specs=[st_spec, pl.BlockSpec((d_inner, nb), lambda i: (0, 0))],
        out_shape=[jax.ShapeDtypeStruct(st.shape, F32), jax.ShapeDtypeStruct((d_inner, nb), F32)],
        compiler_params=_cparams(("arbitrary",), vmem),
        name="ssd_sample_state",
    )(dec, st, xdt_t, bdt, c_t)


def _ssd_sample_post_kernel(y_t_ref, xs_ref, z_ref, dskip_ref, ng_ref, y_ref):
    y = y_t_ref[...].T + dskip_ref[...] * xs_ref[...]
    y_ref[...] = _ssd_gate_norm(y, z_ref[...], ng_ref[...]).astype(y_ref.dtype)


def _ssd_sample_post(y_t, xs, z, dskip, ng):
    args = [y_t, xs, z, dskip, ng]
    vmem = 2 * sum(_nbytes(a.shape, a.dtype) for a in args) + 8 * _nbytes(xs.shape, F32)
    return pl.pallas_call(
        _ssd_sample_post_kernel,
        grid=(1,),
        in_specs=[_resident(a.shape) for a in args],
        out_specs=pl.BlockSpec(xs.shape, lambda i: (0, 0)),
        out_shape=jax.ShapeDtypeStruct(xs.shape, F32),
        compiler_params=_cparams(("arbitrary",), vmem),
        name="ssd_sample_post",
    )(*args)


def _attn_sample_kernel(q_ref, qt_ref, k_ref, v_ref, c0_ref, c1_ref, c2_ref, o_ref):
    q_rows = q_ref[0] * ATTN_SCALE
    q_cols = qt_ref[0] * ATTN_SCALE
    s_new_all = jnp.sum(k_ref[0] * q_rows, axis=-1, keepdims=True)
    scores = []
    for gi, (c_ref, (_, dil)) in enumerate(zip((c0_ref, c1_ref, c2_ref), DIL_GROUPS)):
        rows = []
        for h in range(HPG):
            col = gi * HPG + h
            rows.append(jnp.sum(c_ref[0, 0, h] * q_cols[:, col:col + 1], axis=0, keepdims=True))
        s = jnp.concatenate(rows, axis=0)
        pos = lax.broadcasted_iota(jnp.int32, s.shape, 1)
        scores.append((jnp.where(pos % dil == 0, s, -jnp.inf), s_new_all[gi * HPG:(gi + 1) * HPG]))
    m = functools.reduce(jnp.maximum, [jnp.maximum(jnp.max(s, axis=1, keepdims=True), sn) for s, sn in scores])
    den = jnp.zeros((HPG, 1), F32)
    o_new = jnp.zeros((HPG, HEAD_DIM), F32)
    o_cache_t = jnp.zeros((HEAD_DIM, LANES), F32)
    lane = lax.broadcasted_iota(jnp.int32, (HEAD_DIM, LANES), 1)
    for gi, ((s, sn), c_ref) in enumerate(zip(scores, (c0_ref, c1_ref, c2_ref))):
        p = jnp.exp(s - m)
        p_n = jnp.exp(sn - m)
        den = den + jnp.sum(p, axis=1, keepdims=True) + p_n
        o_new = o_new + p_n * v_ref[0, gi * HPG:(gi + 1) * HPG, :]
        for h in range(HPG):
            contrib = jnp.sum(c_ref[0, 1, h] * p[h:h + 1, :], axis=1, keepdims=True)
            o_cache_t = jnp.where(lane == h, o_cache_t + contrib, o_cache_t)
    o_cache = jnp.concatenate([o_cache_t, jnp.zeros_like(o_cache_t)], axis=0).T
    o_ref[0] = (o_cache[:HPG, :HEAD_DIM] + o_new) / den


def _attn_sample(q, k, v, caches):
    nb, n_heads, _ = q.shape
    views, specs = [], []
    for (window, dil), c in zip(DIL_GROUPS, caches):
        assert c.shape[1] == window and window // dil == NK
        views.append(jnp.transpose(c, (0, 2, 3, 4, 1)))
        specs.append(pl.BlockSpec((1, 2, HPG, HEAD_DIM, window), lambda i: (i, 0, 0, 0, 0)))
    vec = pl.BlockSpec((1, n_heads, HEAD_DIM), lambda i: (i, 0, 0))
    vec_t = pl.BlockSpec((1, HEAD_DIM, n_heads), lambda i: (i, 0, 0))
    cache_bytes = sum(_nbytes((2, HPG, HEAD_DIM, window), F32) for window, _ in DIL_GROUPS)
    vmem = 2 * cache_bytes + 6 * _nbytes((HEAD_DIM, DIL_GROUPS[-1][0]), F32) + 8 * _nbytes((LANES, LANES), F32)
    return pl.pallas_call(
        _attn_sample_kernel,
        grid=(nb,),
        in_specs=[vec, vec_t, vec, vec] + specs,
        out_specs=pl.BlockSpec((1, HPG, HEAD_DIM), lambda i: (i, 0, 0)),
        out_shape=jax.ShapeDtypeStruct((nb, HPG, HEAD_DIM), F32),
        compiler_params=_cparams(("parallel",), vmem),
        name="attn_sample",
    )(q, jnp.swapaxes(q, 1, 2), k, v, *views)


def _row(v, width=None):
    v = v.astype(F32).reshape(1, -1)
    if width is not None and v.shape[1] < width:
        v = jnp.pad(v, ((0, 0), (0, width - v.shape[1])))
    return v


def _layer_weights(prm):
    d_model = prm['w_in'].shape[0]
    d_inner = prm['norm_ssm'].shape[0]
    n_heads = d_inner // HEAD_DIM
    conv_dim = prm['conv_w'].shape[1]
    attn_w = len(DIL_GROUPS) * HPG * HEAD_DIM
    splits = [d_inner, conv_dim, n_heads, attn_w, attn_w, attn_w, d_model, d_model]
    offs = [0]
    for n in splits:
        offs.append(offs[-1] + n)
    w_in = prm['w_in']
    seg = [w_in[:, offs[j]:offs[j + 1]].astype(BF16) for j in range(len(splits))]
    seg[2] = jnp.pad(seg[2], ((0, 0), (0, LANES - n_heads)))
    bf = lambda name: prm[name].astype(BF16)
    return dict(
        ffn1=(_row(prm['norm_ffn1']), bf('w_ffn1_gate'), bf('w_ffn1_up'), bf('w_ffn1_down'), _row(prm['norm_mix'])),
        proj_ssm=seg[0:3], proj_attn=seg[3:8],
        conv_w=prm['conv_w'].astype(F32), conv_b=_row(prm['conv_b']),
        dt_bias=_row(prm['dt_bias'], LANES), a_log=_row(prm['a_log'], LANES),
        d_skip=_row(jnp.repeat(prm['d_skip'], HEAD_DIM)), norm_ssm=_row(prm['norm_ssm']),
        out=(bf('w_o_ssm'), bf('w_o_attn'), bf('w_out'), _row(prm['norm_ffn2']), bf('w_ffn2_gate'),
             bf('w_ffn2_up'), bf('w_ffn2_down'), _row(prm['norm_ple']), bf('w_ple_gate'), bf('w_ple_proj')),
        d_inner=d_inner, conv_dim=conv_dim,
    )


def _kv_stack(k, v, gi, keep):
    b, s, _ = k.shape
    gw = HPG * HEAD_DIM
    sel = lambda t: t[:, s - keep:, gi * gw:(gi + 1) * gw].astype(F32).reshape(b, keep, HPG, HEAD_DIM)
    return jnp.stack([sel(k), sel(v)], axis=2)


def _kv_stack_wide(k, v, batch, dil, keep):
    rows = k.shape[0] // batch
    sel = lambda t: (t.reshape(batch, rows, dil * GROUP_WIDTH)[:, rows - keep // dil:, :]
                     .astype(F32).reshape(batch, keep, HPG, HEAD_DIM))
    return jnp.stack([sel(k), sel(v)], axis=2)


def _prompt_layer(x, p_emb, w, norm_final, tm):
    b, s, d = x.shape
    t = b * s
    h1, u = _ffn(x.reshape(t, d), *w['ffn1'], tm)
    z, xbc, dt = _proj(_proj_ssm_kernel, u, w['proj_ssm'], (BF16, BF16, F32), tm, "proj_ssm")
    qs, ks, vs, gs, ga = _proj_attn_wide(u, w['proj_attn'], s, tm)
    r3 = lambda a: a.reshape(b, s, a.shape[-1])
    xbc3 = r3(xbc)
    y_ssm, ssm_new = _ssd_prompt(r3(z), xbc3, r3(dt), w['conv_w'], w['conv_b'], w['dt_bias'], w['a_log'],
                                 w['d_skip'], w['norm_ssm'])
    order = sorted(range(len(DIL_GROUPS)), key=lambda gi: -DIL_GROUPS[gi][1])
    assert all(DIL_GROUPS[a][1] == MERGE_FAN * DIL_GROUPS[c][1] for a, c in zip(order, order[1:]))
    assert DIL_GROUPS[order[-1]][1] == 1
    o, lse = None, None
    for j, gi in enumerate(order):
        o, lse = _attn_prompt_group(qs[gi], ks[gi], vs[gi], b, gi, j == 0, j == len(order) - 1, o, lse)
    y = _out_stage(h1, y_ssm.reshape(t, -1), o, gs, ga, p_emb.reshape(t, -1),
                   w['out'] + (_row(norm_final),), OUT_ROW_TILE)
    kv = [_kv_stack_wide(ks[gi], vs[gi], b, dil, min(window, s)) for gi, (window, dil) in enumerate(DIL_GROUPS)]
    conv_new = xbc3[:, s - (CONV_W - 1):, :].astype(F32)
    return y.reshape(b, s, d), kv, conv_new, ssm_new


def _sample_layer(x, p_emb, w, norm_final, conv_prev, ssm_prev, caches):
    nb, s, d = x.shape
    assert s == 1
    h1, u = _ffn(x.reshape(nb, d), *w['ffn1'], nb)
    z, xbc, dt = _proj(_proj_ssm_kernel, u, w['proj_ssm'], (F32, F32, F32), nb, "proj_ssm_sample")
    q, k, v, gs, ga = _proj(functools.partial(_proj_attn_kernel, seq=1, pos0=PAST_LEN), u, w['proj_attn'],
                            (F32,) * 5, nb, "proj_attn_sample")
    xs, xdt_t, bdt, c_t, dec = _ssd_sample_prep(xbc, conv_prev.reshape(nb, -1), dt, w['conv_w'], w['conv_b'],
                                                w['dt_bias'], w['a_log'], w['d_inner'])
    dec_flat = dec[:, :ssm_prev.shape[1]].reshape(-1)
    ssm_new, y_t = _ssd_sample_state(dec_flat, ssm_prev, xdt_t, bdt, c_t)
    y_ssm = _ssd_sample_post(y_t, xs, z, w['d_skip'], w['norm_ssm'])
    by_head = lambda a: a.reshape(nb, -1, HEAD_DIM)
    attn = _attn_sample(by_head(q), by_head(k), by_head(v), caches)
    y = _out_stage(h1, y_ssm, attn.reshape(nb, -1), gs, ga, p_emb.reshape(nb, -1),
                   w['out'] + (_row(norm_final),), nb)
    r3 = lambda a: a.reshape(nb, 1, a.shape[-1])
    kv = [_kv_stack(r3(k), r3(v), gi, 1) for gi in range(len(DIL_GROUPS))]
    conv_new = jnp.concatenate([conv_prev[:, 1:], xbc[:, None, :]], axis=1)
    return y.reshape(nb, 1, d), kv, conv_new, ssm_new


PROMPT_ROW_TILE = 512
OUT_ROW_TILE = 512


def kernel(x_prompt, x_sample, cache_kv_w128, cache_kv_w512, cache_kv_w2048, state_conv, state_ssm, p_prompt, p_sample, norm_ffn1, w_ffn1_gate, w_ffn1_up, w_ffn1_down, norm_mix, w_in, conv_w, conv_b, dt_bias, a_log, d_skip, norm_ssm, w_o_ssm, w_o_attn, w_out, norm_ffn2, w_ffn2_gate, w_ffn2_up, w_ffn2_down, norm_ple, w_ple_gate, w_ple_proj, norm_final):
    depth = w_in.shape[0]
    assert depth == 1
    layer_params = dict(
        norm_ffn1=norm_ffn1, w_ffn1_gate=w_ffn1_gate, w_ffn1_up=w_ffn1_up, w_ffn1_down=w_ffn1_down,
        norm_mix=norm_mix, w_in=w_in, conv_w=conv_w, conv_b=conv_b, dt_bias=dt_bias, a_log=a_log,
        d_skip=d_skip, norm_ssm=norm_ssm, w_o_ssm=w_o_ssm, w_o_attn=w_o_attn, w_out=w_out,
        norm_ffn2=norm_ffn2, w_ffn2_gate=w_ffn2_gate, w_ffn2_up=w_ffn2_up, w_ffn2_down=w_ffn2_down,
        norm_ple=norm_ple, w_ple_gate=w_ple_gate, w_ple_proj=w_ple_proj)
    i = 0
    w = _layer_weights({name: val[i] for name, val in layer_params.items()})
    yp, kvp, convp, ssmp = _prompt_layer(x_prompt, p_prompt[i], w, norm_final, PROMPT_ROW_TILE)
    ys, kvs, convs, ssms = _sample_layer(x_sample, p_sample[i], w, norm_final, state_conv[i], state_ssm[i],
                                         (cache_kv_w128[i], cache_kv_w512[i], cache_kv_w2048[i]))
    st = lambda a: a[None]
    return (yp, ys, st(kvp[0]), st(kvp[1]), st(kvp[2]), st(convp), st(ssmp),
            st(kvs[0]), st(kvs[1]), st(kvs[2]), st(convs), st(ssms))
```

```python
import functools
import math

import jax
import jax.numpy as jnp
from jax import lax
from jax.experimental import pallas as pl
from jax.experimental.pallas import tpu as pltpu

F32 = jnp.float32
BF16 = jnp.bfloat16

EPS = 1e-6
ROPE_THETA = 10000.0
PAST_LEN = 8192
HEAD_DIM = 64
HPG = 8
SSM_GROUPS = 4
SSM_STATE = 128
CONV_W = 4
DIL_GROUPS = ((128, 1), (512, 4), (2048, 16))
ATTN_SCALE = HEAD_DIM ** -0.5
NK = 128

LANES = 128
SUBLANES = 8
VMEM_LIMIT_CAP = 56 * 1024 * 1024


def _cparams(sem, vmem_bytes):
    return pltpu.CompilerParams(dimension_semantics=sem,
                                vmem_limit_bytes=int(min(VMEM_LIMIT_CAP, vmem_bytes)))


def _resident(shape):
    nd = len(shape)
    return pl.BlockSpec(shape, lambda *_: (0,) * nd, pipeline_mode=pl.Buffered(1))


def _nbytes(shape, dtype):
    return math.prod(shape) * jnp.dtype(dtype).itemsize


def _rms(x, gain):
    ms = jnp.mean(x * x, axis=-1, keepdims=True)
    return x * lax.rsqrt(ms + EPS) * gain


def _sigmoid(x):
    return 1.0 / (1.0 + jnp.exp(-x))


def _silu(x):
    half = 0.5 * x
    return half + half * jnp.tanh(half)


def _softplus(x):
    return jnp.maximum(x, 0.0) + jnp.log1p(jnp.exp(-jnp.abs(x)))


def _dot(a, b):
    return jnp.dot(a, b, preferred_element_type=F32)


def _dot_nt(a, b):
    return lax.dot_general(a, b, (((1,), (1,)), ((), ())), preferred_element_type=F32)


def _split3(x):
    hi = x.astype(BF16)
    r1 = x - hi.astype(F32)
    mid = r1.astype(BF16)
    lo = (r1 - mid.astype(F32)).astype(BF16)
    return hi, mid, lo


def _dot_sel_l(sel, x):
    hi, mid, lo = _split3(x)
    return _dot(sel, hi) + _dot(sel, mid) + _dot(sel, lo)


def _dot_sel_r(x, sel):
    hi, mid, lo = _split3(x)
    return _dot(hi, sel) + _dot(mid, sel) + _dot(lo, sel)


def _head_expand(n_heads_padded, n_heads):
    r = lax.broadcasted_iota(jnp.int32, (n_heads_padded, n_heads * HEAD_DIM), 0)
    c = lax.broadcasted_iota(jnp.int32, (n_heads_padded, n_heads * HEAD_DIM), 1)
    return (c // HEAD_DIM == r).astype(BF16)


FF_CHUNK = 256


def _swiglu_half(x, gain, wg_ref, wu_ref, wd_ref):
    u = _rms(x, gain).astype(BF16)
    d_ff = wg_ref.shape[1]
    acc = jnp.zeros(x.shape, F32)
    for c in range(d_ff // FF_CHUNK):
        sl = slice(c * FF_CHUNK, (c + 1) * FF_CHUNK)
        g = _dot(u, wg_ref[:, sl])
        up = _dot(u, wu_ref[:, sl])
        acc = acc + _dot((_silu(g) * up).astype(BF16), wd_ref[sl, :])
    return x + 0.5 * acc


def _ffn_kernel(x_ref, n1_ref, wg_ref, wu_ref, wd_ref, n2_ref, h_ref, u_ref):
    h = _swiglu_half(x_ref[...], n1_ref[...], wg_ref, wu_ref, wd_ref)
    h_ref[...] = h
    u_ref[...] = _rms(h, n2_ref[...]).astype(u_ref.dtype)


def _ffn(x, n1, wg, wu, wd, n2, tm):
    t, d = x.shape
    d_ff = wg.shape[1]
    vmem = (3 * _nbytes((d, d_ff), BF16) + 2 * 2 * _nbytes((tm, d), F32) + 2 * _nbytes((tm, d), BF16)
            + 6 * _nbytes((tm, d), F32) + 4 * _nbytes((tm, FF_CHUNK), F32))
    return pl.pallas_call(
        _ffn_kernel,
        grid=(t // tm,),
        in_specs=[pl.BlockSpec((tm, d), lambda i: (i, 0)),
                  _resident((1, d)), _resident((d, d_ff)), _resident((d, d_ff)), _resident((d_ff, d)),
                  _resident((1, d))],
        out_specs=[pl.BlockSpec((tm, d), lambda i: (i, 0)), pl.BlockSpec((tm, d), lambda i: (i, 0))],
        out_shape=[jax.ShapeDtypeStruct((t, d), F32), jax.ShapeDtypeStruct((t, d), BF16)],
        compiler_params=_cparams(("parallel",), vmem),
        name="ffn_half_step",
    )(x, n1, wg, wu, wd, n2)


def _rope_tables(tm, seq, pos0):
    row = lax.broadcasted_iota(jnp.int32, (tm, LANES), 0) + pl.program_id(0) * tm
    pos = (row % seq + pos0).astype(F32)
    lane = lax.broadcasted_iota(jnp.int32, (tm, LANES), 1)
    half = HEAD_DIM // 2
    j = (lane % half).astype(F32)
    inv_freq = jnp.exp(j * (-math.log(ROPE_THETA) / half))
    ang = pos * inv_freq
    first = (lane % HEAD_DIM) < half
    return jnp.cos(ang), jnp.where(first, -jnp.sin(ang), jnp.sin(ang)), first


def _rotary(x, cos, sin_signed, first):
    outs = []
    for c in range(x.shape[1] // LANES):
        xb = x[:, c * LANES:(c + 1) * LANES]
        other = jnp.where(first, pltpu.roll(xb, LANES - HEAD_DIM // 2, 1), pltpu.roll(xb, HEAD_DIM // 2, 1))
        outs.append(xb * cos + other * sin_signed)
    return jnp.concatenate(outs, axis=1)


def _proj_ssm_kernel(u_ref, wz_ref, wx_ref, wdt_ref, z_ref, xbc_ref, dt_ref):
    u = u_ref[...]
    z_ref[...] = _dot(u, wz_ref[...]).astype(z_ref.dtype)
    xbc_ref[...] = _dot(u, wx_ref[...]).astype(xbc_ref.dtype)
    dt_ref[...] = _dot(u, wdt_ref[...])


def _proj_attn_kernel(u_ref, wq_ref, wk_ref, wv_ref, wgs_ref, wga_ref,
                      q_ref, k_ref, v_ref, gs_ref, ga_ref, *, seq, pos0):
    u = u_ref[...]
    cos, sin_signed, first = _rope_tables(u.shape[0], seq, pos0)
    q_ref[...] = _rotary(_dot(u, wq_ref[...]), cos, sin_signed, first).astype(q_ref.dtype)
    k_ref[...] = _rotary(_dot(u, wk_ref[...]), cos, sin_signed, first).astype(k_ref.dtype)
    v_ref[...] = _dot(u, wv_ref[...]).astype(v_ref.dtype)
    gs_ref[...] = _sigmoid(_dot(u, wgs_ref[...])).astype(gs_ref.dtype)
    ga_ref[...] = _sigmoid(_dot(u, wga_ref[...])).astype(ga_ref.dtype)


GROUP_WIDTH = HPG * HEAD_DIM
_STAGED_GROUPS = [gi for gi, (_, dil) in enumerate(DIL_GROUPS) if dil > 1]


ROW_SLAB = 128


def _rope_store_wide(res, rope, cos_ref, sin_ref, first, stage, out_ref, dil):
    tm = res.shape[0]
    for rb in range(tm // ROW_SLAB):
        rows = slice(rb * ROW_SLAB, (rb + 1) * ROW_SLAB)
        for j in range(GROUP_WIDTH // LANES):
            lanes = slice(j * LANES, (j + 1) * LANES)
            xb = res[rows, lanes]
            if rope:
                other = jnp.where(first, pltpu.roll(xb, LANES - HEAD_DIM // 2, 1), pltpu.roll(xb, HEAD_DIM // 2, 1))
                xb = xb * cos_ref[rows, :] + other * sin_ref[rows, :]
            if dil == 1:
                out_ref[rows, lanes] = xb.astype(out_ref.dtype)
            else:
                stage[j, rows, :] = xb
    if dil > 1:
        n_rows = tm // dil
        for r in range(dil):
            for j in range(GROUP_WIDTH // LANES):
                lo = r * GROUP_WIDTH + j * LANES
                out_ref[:, lo:lo + LANES] = stage[j, pl.ds(r, n_rows, stride=dil), :].astype(out_ref.dtype)


def _rope_table_kernel(cos_ref, sin_ref, *, seq):
    cos, sin_signed, _ = _rope_tables(cos_ref.shape[0], seq, 0)
    cos_ref[...] = cos
    sin_ref[...] = sin_signed


def _rope_table(seq, tm):
    spec = pl.BlockSpec((tm, LANES), lambda i: (i, 0))
    shape = jax.ShapeDtypeStruct((seq, LANES), F32)
    return pl.pallas_call(
        functools.partial(_rope_table_kernel, seq=seq),
        grid=(seq // tm,), in_specs=[], out_specs=[spec, spec], out_shape=[shape, shape],
        compiler_params=_cparams(("parallel",), 16 * _nbytes((tm, LANES), F32)),
        name="rope_table",
    )()


def _proj_attn_wide_kernel(u_ref, cos_ref, sin_ref, wq_ref, wk_ref, wv_ref, wgs_ref, wga_ref, *refs):
    n_grp = len(DIL_GROUPS)
    q_refs, k_refs, v_refs = refs[0:n_grp], refs[n_grp:2 * n_grp], refs[2 * n_grp:3 * n_grp]
    gs_ref, ga_ref, stage = refs[3 * n_grp:]
    u = u_ref[...]
    first = lax.broadcasted_iota(jnp.int32, (ROW_SLAB, LANES), 1) % HEAD_DIM < HEAD_DIM // 2
    for ti, (w_ref, out_refs, rope) in enumerate(((wq_ref, q_refs, True), (wk_ref, k_refs, True),
                                                 (wv_ref, v_refs, False))):
        for gi, (_, dil) in enumerate(DIL_GROUPS):
            res = _dot(u, w_ref[:, gi * GROUP_WIDTH:(gi + 1) * GROUP_WIDTH])
            slot = None if dil == 1 else stage.at[ti * len(_STAGED_GROUPS) + _STAGED_GROUPS.index(gi)]
            _rope_store_wide(res, rope, cos_ref, sin_ref, first, slot, out_refs[gi], dil)
    gs_ref[...] = _sigmoid(_dot(u, wgs_ref[...])).astype(gs_ref.dtype)
    ga_ref[...] = _sigmoid(_dot(u, wga_ref[...])).astype(ga_ref.dtype)


def _proj_attn_wide(u, weights, seq, tm):
    t, d = u.shape
    d_model = weights[3].shape[1]
    row = lambda i: (i, 0)
    qkv_specs = [pl.BlockSpec((tm // dil, dil * GROUP_WIDTH), row) for _, dil in DIL_GROUPS] * 3
    qkv_shapes = [jax.ShapeDtypeStruct((t // dil, dil * GROUP_WIDTH), BF16) for _, dil in DIL_GROUPS] * 3
    n_stage = 3 * len(_STAGED_GROUPS)
    vmem = (sum(_nbytes(w.shape, BF16) for w in weights) + 2 * _nbytes((tm, d), BF16)
            + 2 * _nbytes((tm, 3 * len(DIL_GROUPS) * GROUP_WIDTH + 2 * d_model), BF16)
            + 4 * _nbytes((tm, LANES), F32)
            + n_stage * _nbytes((tm, GROUP_WIDTH), F32) + 6 * _nbytes((tm, weights[0].shape[1]), F32))
    cos, sin_signed = _rope_table(seq, tm)
    tiles_per_seq = seq // tm
    table = pl.BlockSpec((tm, LANES), lambda i: (i % tiles_per_seq, 0))
    outs = pl.pallas_call(
        _proj_attn_wide_kernel,
        grid=(t // tm,),
        in_specs=[pl.BlockSpec((tm, d), row), table, table] + [_resident(w.shape) for w in weights],
        out_specs=qkv_specs + [pl.BlockSpec((tm, d_model), row)] * 2,
        out_shape=qkv_shapes + [jax.ShapeDtypeStruct((t, d_model), BF16)] * 2,
        scratch_shapes=[pltpu.VMEM((n_stage, GROUP_WIDTH // LANES, tm, LANES), F32)],
        compiler_params=_cparams(("parallel",), vmem),
        name="proj_attn",
    )(u, cos, sin_signed, *weights)
    n_grp = len(DIL_GROUPS)
    return outs[0:n_grp], outs[n_grp:2 * n_grp], outs[2 * n_grp:3 * n_grp], outs[3 * n_grp], outs[3 * n_grp + 1]


def _proj(kernel, u, weights, out_dtypes, tm, name):
    t, d = u.shape
    row = lambda i: (i, 0)
    widths = [w.shape[1] for w in weights]
    vmem = (sum(_nbytes(w.shape, BF16) for w in weights) + 2 * _nbytes((tm, d), BF16)
            + sum(2 * _nbytes((tm, n), dt) for n, dt in zip(widths, out_dtypes))
            + 4 * _nbytes((tm, max(widths)), F32))
    return pl.pallas_call(
        kernel,
        grid=(t // tm,),
        in_specs=[pl.BlockSpec((tm, d), row)] + [_resident(w.shape) for w in weights],
        out_specs=[pl.BlockSpec((tm, n), row) for n in widths],
        out_shape=[jax.ShapeDtypeStruct((t, n), dt) for n, dt in zip(widths, out_dtypes)],
        compiler_params=_cparams(("parallel",), vmem),
        name=name,
    )(u, *weights)


SSD_CHUNK = 128
SSD_CHUNKS_PER_STEP = 4


def _ssd_gate_norm(y, z, gain):
    return _rms(y * _silu(z), gain)


def _ssd_chunk(x_prev, x_cur, z, dt_raw, states, cw_ref, cb_ref, dtb_ref, alog_ref, dskip_ref, ng_ref,
               shift_ref, expand_ref):
    L = SSD_CHUNK
    d_inner = z.shape[-1]
    gw = HPG * HEAD_DIM
    gn = SSM_GROUPS * SSM_STATE

    both = jnp.concatenate([x_prev, x_cur], axis=0)
    conv = cb_ref[...] + x_cur.astype(F32) * cw_ref[CONV_W - 1:CONV_W, :]
    shifted = _dot(shift_ref[...], both)
    for back in range(1, CONV_W):
        tap = CONV_W - 1 - back
        conv = conv + shifted[(back - 1) * L:back * L, :] * cw_ref[tap:tap + 1, :]
    xa = _silu(conv)
    xs = xa[:, :d_inner]
    bm = xa[:, d_inner:d_inner + gn]
    cm = xa[:, d_inner + gn:]

    dt = _softplus(dt_raw + dtb_ref[...])
    a = -jnp.exp(alog_ref[...])
    row = lax.broadcasted_iota(jnp.int32, (L, L), 0)
    col = lax.broadcasted_iota(jnp.int32, (L, L), 1)
    causal = row >= col
    acs = _dot_sel_l(causal.astype(BF16), dt * a)
    acs_t = acs.T
    dt_t = dt.T
    acs_last = acs[L - 1:L, :]

    expand = expand_ref[...]
    per_head = jnp.concatenate([jnp.exp(acs), jnp.exp(acs_last - acs) * dt], axis=0)
    per_lane = _dot(per_head.astype(BF16), expand)
    e_acs, w_end = per_lane[:L], per_lane[L:]
    dec_state = _dot_sel_r(jnp.broadcast_to(jnp.exp(acs_last), (SUBLANES, LANES)), expand)[0:1, :]

    xs_b = xs.astype(BF16)
    xw = (xs * w_end).astype(BF16)
    lane_first = lax.broadcasted_iota(jnp.int32, (L, LANES), 1) < HEAD_DIM
    y_groups, new_states = [], []
    for g in range(SSM_GROUPS):
        gsl = slice(g * gw, (g + 1) * gw)
        b_g = bm[:, g * SSM_STATE:(g + 1) * SSM_STATE]
        c_g = cm[:, g * SSM_STATE:(g + 1) * SSM_STATE].astype(BF16)
        cb = _dot_nt(c_g, b_g.astype(BF16))
        st_g = states[g]
        y_inter = _dot(c_g, st_g.astype(BF16))
        pairs = []
        for pr in range(HPG // 2):
            w_pair = []
            for sub in range(2):
                h = g * HPG + pr * 2 + sub
                seg = acs[:, h:h + 1] - acs_t[h:h + 1, :]
                w = cb * jnp.exp(jnp.where(causal, seg, -jnp.inf)) * dt_t[h:h + 1, :]
                w_pair.append(w.astype(BF16))
            lo = g * gw + pr * LANES
            res = _dot(jnp.concatenate(w_pair, axis=0), xs_b[:, lo:lo + LANES])
            pairs.append(jnp.where(lane_first, res[:L], res[L:]))
        y_groups.append(jnp.concatenate(pairs, axis=1) + y_inter * e_acs[:, gsl])
        new_states.append(dec_state[:, gsl] * st_g + _dot(b_g.T.astype(BF16), xw[:, gsl]))
    y = jnp.concatenate(y_groups, axis=1) + dskip_ref[...] * xs
    return _ssd_gate_norm(y, z.astype(F32), ng_ref[...]), new_states


def _ssd_prompt_kernel(z_ref, xprev_ref, xbc_ref, dt_ref, cw_ref, cb_ref, dtb_ref, alog_ref, dskip_ref, ng_ref,
                       shift_ref, expand_ref, y_ref, st_ref, st_t):
    L = SSD_CHUNK
    c = pl.program_id(1)

    @pl.when(c == 0)
    def _():
        st_t[...] = jnp.zeros(st_t.shape, F32)

    states = [st_t[g] for g in range(SSM_GROUPS)]
    x_prev = jnp.where(c > 0, xprev_ref[0], jnp.zeros(xprev_ref.shape[1:], xprev_ref.dtype))
    for j in range(xbc_ref.shape[1] // L):
        rows = slice(j * L, (j + 1) * L)
        x_cur = xbc_ref[0, rows, :]
        y, states = _ssd_chunk(x_prev, x_cur, z_ref[0, rows, :], dt_ref[0, rows, :], states,
                               cw_ref, cb_ref, dtb_ref, alog_ref, dskip_ref, ng_ref, shift_ref, expand_ref)
        y_ref[0, rows, :] = y.astype(y_ref.dtype)
        x_prev = x_cur
    for g in range(SSM_GROUPS):
        st_t[g] = states[g]

    @pl.when(c == pl.num_programs(1) - 1)
    def _():
        for g in range(SSM_GROUPS):
            st_ref[0, g * HPG:(g + 1) * HPG] = states[g].T.reshape(HPG, HEAD_DIM, SSM_STATE)


def _ssd_prompt(z, xbc, dt, cw, cb, dtb, alog, dskip, ng):
    b, s, d_inner = z.shape
    conv_dim = xbc.shape[-1]
    n_heads = d_inner // HEAD_DIM
    L = SSD_CHUNK
    per_step = math.gcd(s // L, SSD_CHUNKS_PER_STEP)
    blk = lambda n: pl.BlockSpec((1, per_step * L, n), lambda i, c: (i, c, 0))
    r = jnp.arange(L)[:, None]
    cidx = jnp.arange(2 * L)[None, :]
    shift = jnp.concatenate([(cidx == r + L - k) for k in range(1, CONV_W)], axis=0).astype(BF16)
    heads = jnp.arange(LANES)[:, None]
    expand = (jnp.arange(d_inner)[None, :] // HEAD_DIM == heads).astype(BF16)
    prev_blk = pl.BlockSpec((1, L, conv_dim), lambda i, c: (i, jnp.maximum(c * per_step - 1, 0), 0))
    vmem = (2 * per_step * (_nbytes((L, d_inner), z.dtype) * 2 + _nbytes((L, conv_dim), xbc.dtype)
                            + _nbytes((L, LANES), F32))
            + 2 * _nbytes((L, conv_dim), xbc.dtype)
            + 2 * _nbytes((n_heads, HEAD_DIM, SSM_STATE), F32)
            + _nbytes((SSM_GROUPS, SSM_STATE, HPG * HEAD_DIM), F32)
            + _nbytes(shift.shape, BF16) + _nbytes(expand.shape, BF16)
            + 16 * per_step * _nbytes((L, conv_dim), F32))
    return pl.pallas_call(
        _ssd_prompt_kernel,
        grid=(b, s // (per_step * L)),
        in_specs=[blk(d_inner), prev_blk, blk(conv_dim), blk(LANES),
                  _resident(cw.shape), _resident(cb.shape), _resident(dtb.shape), _resident(alog.shape),
                  _resident(dskip.shape), _resident(ng.shape), _resident(shift.shape), _resident(expand.shape)],
        out_specs=[blk(d_inner),
                   pl.BlockSpec((1, n_heads, HEAD_DIM, SSM_STATE), lambda i, c: (i, 0, 0, 0))],
        out_shape=[jax.ShapeDtypeStruct((b, s, d_inner), BF16),
                   jax.ShapeDtypeStruct((b, n_heads, HEAD_DIM, SSM_STATE), F32)],
        scratch_shapes=[pltpu.VMEM((SSM_GROUPS, SSM_STATE, HPG * HEAD_DIM), F32)],
        compiler_params=_cparams(("parallel", "arbitrary"), vmem),
        name="ssd_prompt_scan",
    )(z, xbc, xbc, dt, cw, cb, dtb, alog, dskip, ng, shift, expand)


MERGE_FAN = 4
ATTN_BLOCKS_PER_STEP = 8


def _attn_prompt_kernel(*refs, first, last):
    q_ref, kp_ref, kc_ref, vp_ref, vc_ref = refs[:5]
    refs = refs[5:]
    if not first:
        oin_refs, lin_refs = refs[:MERGE_FAN], refs[MERGE_FAN:2 * MERGE_FAN]
        refs = refs[2 * MERGE_FAN:]
    out_refs = refs
    o_ref = out_refs[0]
    n = pl.program_id(2)
    n_blocks = q_ref.shape[0] // NK
    qi = lax.broadcasted_iota(jnp.int32, (NK, 2 * NK), 0)
    kj = lax.broadcasted_iota(jnp.int32, (NK, 2 * NK), 1)
    dist = qi + NK - kj
    band = (dist >= 0) & (dist <= NK)
    lane = lax.broadcasted_iota(jnp.int32, (NK, LANES), 1)
    lane_first = lane < HEAD_DIM
    if not first:
        rows = NK // MERGE_FAN
        dst = lax.broadcasted_iota(jnp.int32, (NK, NK), 0)
        src = lax.broadcasted_iota(jnp.int32, (NK, NK), 1)
        perm = (src == (dst % MERGE_FAN) * rows + dst // MERGE_FAN).astype(BF16)
        expand = _head_expand(LANES, HPG)
    for j in range(n_blocks):
        own = slice(j * NK, (j + 1) * NK)
        q = q_ref[own, :] * ATTN_SCALE
        if j == 0:
            k_before, v_before = kp_ref[...], vp_ref[...]
            mask = band & ((kj >= NK) | (n > 0))
        else:
            before = slice((j - 1) * NK, j * NK)
            k_before, v_before = kc_ref[before, :], vc_ref[before, :]
            mask = band
        kk = jnp.concatenate([k_before, kc_ref[own, :]], axis=0)
        vv = jnp.concatenate([v_before, vc_ref[own, :]], axis=0)
        lse_out = jnp.zeros((NK, LANES), F32)
        o_parts = []
        for pr in range(HPG // 2):
            sl = slice(pr * LANES, (pr + 1) * LANES)
            q_pair, k_pair, v_pair = q[:, sl], kk[:, sl], vv[:, sl]
            res = []
            for sub in range(2):
                h = pr * 2 + sub
                qm = jnp.where(lane_first if sub == 0 else ~lane_first, q_pair, jnp.zeros_like(q_pair))
                s = jnp.where(mask, _dot_nt(qm, k_pair), -jnp.inf)
                m = jnp.max(s, axis=-1, keepdims=True)
                p = jnp.exp(s - m)
                l = jnp.sum(p, axis=-1, keepdims=True)
                res.append(_dot(p.astype(BF16), v_pair) / l)
                lse_out = jnp.where(lane == h, m + jnp.log(l), lse_out)
            o_parts.append(jnp.where(lane_first, res[0], res[1]))
        o = jnp.concatenate(o_parts, axis=1)
        if not first:
            part = slice(j * rows, (j + 1) * rows)
            o_prev = _dot(perm, jnp.concatenate([r[part, :] for r in oin_refs], axis=0))
            lse_prev = _dot_sel_l(perm, jnp.concatenate([r[part, :] for r in lin_refs], axis=0))
            lse_new = jnp.maximum(lse_out, lse_prev) + jnp.log1p(jnp.exp(-jnp.abs(lse_out - lse_prev)))
            w_prev = _dot_sel_r(jnp.exp(lse_prev - lse_new), expand)
            w_cur = _dot_sel_r(jnp.exp(lse_out - lse_new), expand)
            o = o_prev * w_prev + o * w_cur
            lse_out = lse_new
        o_ref[own, :] = o.astype(o_ref.dtype)
        if not last:
            out_refs[1][own, :] = lse_out


def _attn_prompt_group(q, k, v, batch, gi, first, last, o_in, lse_in):
    window, dil = DIL_GROUPS[gi]
    td = q.shape[0]
    sd = td // batch
    assert window // dil == NK and sd % NK == 0 and q.shape[1] == dil * GROUP_WIDTH
    nb = sd // NK
    g = math.gcd(nb, ATTN_BLOCKS_PER_STEP)
    ns = nb // g
    blk = lambda width: pl.BlockSpec((g * NK, width), lambda i, r, n: (i * ns + n, r))
    prev = pl.BlockSpec((NK, GROUP_WIDTH), lambda i, r, n: (i * nb + jnp.maximum(n * g - 1, 0), r))
    args, in_specs = [q, k, k, v, v], [blk(GROUP_WIDTH), prev, blk(GROUP_WIDTH), prev, blk(GROUP_WIDTH)]
    if not first:
        assert o_in.shape == (td // MERGE_FAN, MERGE_FAN * dil * GROUP_WIDTH)
        piece = lambda width, c: pl.BlockSpec((g * NK // MERGE_FAN, width),
                                              lambda i, r, n: (i * ns + n, dil * c + r))
        args += [o_in] * MERGE_FAN + [lse_in] * MERGE_FAN
        in_specs += [piece(GROUP_WIDTH, c) for c in range(MERGE_FAN)] + [piece(LANES, c) for c in range(MERGE_FAN)]
    out_specs = [blk(GROUP_WIDTH)]
    out_shape = [jax.ShapeDtypeStruct((td, dil * GROUP_WIDTH), BF16)]
    if not last:
        out_specs.append(blk(LANES))
        out_shape.append(jax.ShapeDtypeStruct((td, dil * LANES), F32))
    vmem = (2 * (6 * g + 2) * _nbytes((NK, GROUP_WIDTH), BF16) + 6 * g * _nbytes((NK, LANES), F32)
            + 24 * _nbytes((NK, 2 * NK), F32) + 6 * g * _nbytes((NK, GROUP_WIDTH), F32))
    outs = pl.pallas_call(
        functools.partial(_attn_prompt_kernel, first=first, last=last),
        grid=(batch, dil, ns),
        in_specs=in_specs, out_specs=out_specs, out_shape=out_shape,
        compiler_params=_cparams(("parallel", "parallel", "parallel"), vmem),
        name=f"attn_prompt_w{window}",
    )(*args)
    return outs[0], (None if last else outs[1])


def _out_kernel(h_ref, y_ref, at_ref, gs_ref, ga_ref, p_ref,
                wos_ref, woa_ref, wo_ref, n2_ref, wg_ref, wu_ref, wd_ref, np_ref, wpg_ref, wpp_ref, nf_ref,
                o_ref):
    merged = (gs_ref[...].astype(F32) * _dot(y_ref[...].astype(BF16), wos_ref[...])
              + ga_ref[...].astype(F32) * _dot(at_ref[...].astype(BF16), woa_ref[...]))
    h = h_ref[...] + _dot(merged.astype(BF16), wo_ref[...])
    h = _swiglu_half(h, n2_ref[...], wg_ref, wu_ref, wd_ref)
    gate = _sigmoid(_dot(_rms(h, np_ref[...]).astype(BF16), wpg_ref[...]))
    h = h + gate * _dot(p_ref[...].astype(BF16), wpp_ref[...])
    o_ref[...] = _rms(h, nf_ref[...])


def _out_stage(h, y, attn, gs, ga, p, weights, tm):
    t, d = h.shape
    row = lambda i: (i, 0)
    acts = [h, y, attn, gs, ga, p]
    vmem = (sum(_nbytes(w.shape, w.dtype) for w in weights)
            + sum(2 * _nbytes((tm, a.shape[1]), a.dtype) for a in acts) + 2 * _nbytes((tm, d), F32)
            + 8 * _nbytes((tm, d), F32) + 4 * _nbytes((tm, FF_CHUNK), F32))
    return pl.pallas_call(
        _out_kernel,
        grid=(t // tm,),
        in_specs=[pl.BlockSpec((tm, a.shape[1]), row) for a in acts] + [_resident(w.shape) for w in weights],
        out_specs=pl.BlockSpec((tm, d), row),
        out_shape=jax.ShapeDtypeStruct((t, d), F32),
        compiler_params=_cparams(("parallel",), vmem),
        name="output_stage",
    )(*acts, *weights)


def _ssd_sample_prep_kernel(xn_ref, sc_ref, dtr_ref, cw_ref, cb_ref, dtb_ref, alog_ref,
                            xs_ref, xdt_t_ref, bdt_ref, c_t_ref, dec_ref):
    conv_dim = xn_ref.shape[1]
    d_inner = xs_ref.shape[1]
    gn = SSM_GROUPS * SSM_STATE
    conv = cb_ref[...] + xn_ref[...] * cw_ref[CONV_W - 1:CONV_W, :]
    for tap in range(CONV_W - 1):
        conv = conv + sc_ref[:, tap * conv_dim:(tap + 1) * conv_dim] * cw_ref[tap:tap + 1, :]
    xa = _silu(conv)
    xs = xa[:, :d_inner]
    xs_ref[...] = xs
    dt = _softplus(dtr_ref[...] + dtb_ref[...])
    dec_ref[...] = jnp.exp(dt * -jnp.exp(alog_ref[...]))
    n_heads = d_inner // HEAD_DIM
    dt_x = _dot_sel_r(dt, _head_expand(LANES, n_heads))
    xdt_t_ref[...] = (xs * dt_x).T.astype(xdt_t_ref.dtype)
    bdt_ref[...] = xa[:, d_inner:d_inner + gn]
    for g in range(SSM_GROUPS):
        lo = d_inner + gn + g * SSM_STATE
        c_t_ref[g] = xa[:, lo:lo + SSM_STATE].T.astype(c_t_ref.dtype)


def _ssd_sample_prep(xn, sc, dtr, cw, cb, dtb, alog, d_inner):
    nb, conv_dim = xn.shape
    args = [xn, sc, dtr, cw, cb, dtb, alog]
    out_shape = [jax.ShapeDtypeStruct((nb, d_inner), F32),
                 jax.ShapeDtypeStruct((d_inner, nb), BF16),
                 jax.ShapeDtypeStruct((nb, SSM_GROUPS * SSM_STATE), F32),
                 jax.ShapeDtypeStruct((SSM_GROUPS, SSM_STATE, nb), BF16),
                 jax.ShapeDtypeStruct((nb, LANES), F32)]
    vmem = 2 * sum(_nbytes(a.shape, a.dtype) for a in args) + 2 * sum(_nbytes(o.shape, o.dtype) for o in out_shape) \
        + 8 * _nbytes((nb, conv_dim), F32)
    return pl.pallas_call(
        _ssd_sample_prep_kernel,
        grid=(1,),
        in_specs=[_resident(a.shape) for a in args],
        out_specs=[pl.BlockSpec(o.shape, functools.partial(lambda nd, i: (0,) * nd, len(o.shape))) for o in out_shape],
        out_shape=out_shape,
        compiler_params=_cparams(("arbitrary",), vmem),
        name="ssd_sample_prep",
    )(*args)


SAMPLE_STATE_TILE = 4


def _ssd_sample_state_kernel(dec_ref, st_ref, xdt_t_ref, bdt_ref, c_t_ref, sto_ref, y_t_ref):
    i = pl.program_id(0)
    nb = bdt_ref.shape[0]
    n_heads = st_ref.shape[1]
    gw = HPG * HEAD_DIM

    @pl.when(i == 0)
    def _():
        y_t_ref[...] = jnp.zeros(y_t_ref.shape, F32)

    row = lax.broadcasted_iota(jnp.int32, (nb, SSM_STATE), 0)
    lane = lax.broadcasted_iota(jnp.int32, (gw, nb), 1)
    for j in range(SAMPLE_STATE_TILE):
        b = i * SAMPLE_STATE_TILE + j
        for g in range(SSM_GROUPS):
            b_row = jnp.where(row == b, bdt_ref[:, g * SSM_STATE:(g + 1) * SSM_STATE], 0.0).astype(BF16)
            upd = _dot(xdt_t_ref[g * gw:(g + 1) * gw, :], b_row)
            new = []
            for hg in range(HPG):
                h = g * HPG + hg
                new_h = st_ref[j, h] * dec_ref[b * n_heads + h] + upd[hg * HEAD_DIM:(hg + 1) * HEAD_DIM, :]
                sto_ref[j, h] = new_h
                new.append(new_h)
            yg = _dot(jnp.concatenate(new, axis=0).astype(BF16), c_t_ref[g])
            y_t_ref[g * gw:(g + 1) * gw, :] += jnp.where(lane == b, yg, 0.0)


def _ssd_sample_state(dec, st, xdt_t, bdt, c_t):
    nb, n_heads = st.shape[0], st.shape[1]
    d_inner = n_heads * HEAD_DIM
    bt = SAMPLE_STATE_TILE
    st_spec = pl.BlockSpec((bt, n_heads, HEAD_DIM, SSM_STATE), lambda i: (i, 0, 0, 0))
    vmem = (4 * _nbytes((bt, n_heads, HEAD_DIM, SSM_STATE), F32) + _nbytes(xdt_t.shape, BF16)
            + _nbytes(bdt.shape, F32) + _nbytes(c_t.shape, BF16) + 2 * _nbytes((d_inner, nb), F32)
            + 16 * _nbytes((HPG * HEAD_DIM, SSM_STATE), F32))
    return pl.pallas_call(
        _ssd_sample_state_kernel,
        grid=(nb // bt,),
        in_specs=[pl.BlockSpec(memory_space=pltpu.SMEM), st_spec,
                  _resident(xdt_t.shape), _resident(bdt.shape), _resident(c_t.shape)],
        out_specs=[st_spec, pl.BlockSpec((d_inner, nb), lambda i: (0, 0))],
        out_shape=[jax.ShapeDtypeStruct(st.shape, F32), jax.ShapeDtypeStruct((d_inner, nb), F32)],
        compiler_params=_cparams(("arbitrary",), vmem),
        name="ssd_sample_state",
    )(dec, st, xdt_t, bdt, c_t)


def _ssd_sample_post_kernel(y_t_ref, xs_ref, z_ref, dskip_ref, ng_ref, y_ref):
    y = y_t_ref[...].T + dskip_ref[...] * xs_ref[...]
    y_ref[...] = _ssd_gate_norm(y, z_ref[...], ng_ref[...]).astype(y_ref.dtype)


def _ssd_sample_post(y_t, xs, z, dskip, ng):
    args = [y_t, xs, z, dskip, ng]
    vmem = 2 * sum(_nbytes(a.shape, a.dtype) for a in args) + 8 * _nbytes(xs.shape, F32)
    return pl.pallas_call(
        _ssd_sample_post_kernel,
        grid=(1,),
        in_specs=[_resident(a.shape) for a in args],
        out_specs=pl.BlockSpec(xs.shape, lambda i: (0, 0)),
        out_shape=jax.ShapeDtypeStruct(xs.shape, F32),
        compiler_params=_cparams(("arbitrary",), vmem),
        name="ssd_sample_post",
    )(*args)


def _attn_sample_kernel(q_ref, qt_ref, k_ref, v_ref, c0_ref, c1_ref, c2_ref, o_ref):
    q_rows = q_ref[0] * ATTN_SCALE
    q_cols = qt_ref[0] * ATTN_SCALE
    s_new_all = jnp.sum(k_ref[0] * q_rows, axis=-1, keepdims=True)
    scores = []
    for gi, (c_ref, (_, dil)) in enumerate(zip((c0_ref, c1_ref, c2_ref), DIL_GROUPS)):
        rows = []
        for h in range(HPG):
            col = gi * HPG + h
            rows.append(jnp.sum(c_ref[0, 0, h] * q_cols[:, col:col + 1], axis=0, keepdims=True))
        s = jnp.concatenate(rows, axis=0)
        pos = lax.broadcasted_iota(jnp.int32, s.shape, 1)
        scores.append((jnp.where(pos % dil == 0, s, -jnp.inf), s_new_all[gi * HPG:(gi + 1) * HPG]))
    m = functools.reduce(jnp.maximum, [jnp.maximum(jnp.max(s, axis=1, keepdims=True), sn) for s, sn in scores])
    den = jnp.zeros((HPG, 1), F32)
    o_new = jnp.zeros((HPG, HEAD_DIM), F32)
    o_cache_t = jnp.zeros((HEAD_DIM, LANES), F32)
    lane = lax.broadcasted_iota(jnp.int32, (HEAD_DIM, LANES), 1)
    for gi, ((s, sn), c_ref) in enumerate(zip(scores, (c0_ref, c1_ref, c2_ref))):
        p = jnp.exp(s - m)
        p_n = jnp.exp(sn - m)
        den = den + jnp.sum(p, axis=1, keepdims=True) + p_n
        o_new = o_new + p_n * v_ref[0, gi * HPG:(gi + 1) * HPG, :]
        for h in range(HPG):
            contrib = jnp.sum(c_ref[0, 1, h] * p[h:h + 1, :], axis=1, keepdims=True)
            o_cache_t = jnp.where(lane == h, o_cache_t + contrib, o_cache_t)
    o_cache = jnp.concatenate([o_cache_t, jnp.zeros_like(o_cache_t)], axis=0).T
    o_ref[0] = (o_cache[:HPG, :HEAD_DIM] + o_new) / den


def _attn_sample(q, k, v, caches):
    nb, n_heads, _ = q.shape
    views, specs = [], []
    for (window, dil), c in zip(DIL_GROUPS, caches):
        assert c.shape[1] == window and window // dil == NK
        views.append(jnp.transpose(c, (0, 2, 3, 4, 1)))
        specs.append(pl.BlockSpec((1, 2, HPG, HEAD_DIM, window), lambda i: (i, 0, 0, 0, 0)))
    vec = pl.BlockSpec((1, n_heads, HEAD_DIM), lambda i: (i, 0, 0))
    vec_t = pl.BlockSpec((1, HEAD_DIM, n_heads), lambda i: (i, 0, 0))
    cache_bytes = sum(_nbytes((2, HPG, HEAD_DIM, window), F32) for window, _ in DIL_GROUPS)
    vmem = 2 * cache_bytes + 6 * _nbytes((HEAD_DIM, DIL_GROUPS[-1][0]), F32) + 8 * _nbytes((LANES, LANES), F32)
    return pl.pallas_call(
        _attn_sample_kernel,
        grid=(nb,),
        in_specs=[vec, vec_t, vec, vec] + specs,
        out_specs=pl.BlockSpec((1, HPG, HEAD_DIM), lambda i: (i, 0, 0)),
        out_shape=jax.ShapeDtypeStruct((nb, HPG, HEAD_DIM), F32),
        compiler_params=_cparams(("parallel",), vmem),
        name="attn_sample",
    )(q, jnp.swapaxes(q, 1, 2), k, v, *views)


def _row(v, width=None):
    v = v.astype(F32).reshape(1, -1)
    if width is not None and v.shape[1] < width:
        v = jnp.pad(v, ((0, 0), (0, width - v.shape[1])))
    return v


def _layer_weights(prm):
    d_model = prm['w_in'].shape[0]
    d_inner = prm['norm_ssm'].shape[0]
    n_heads = d_inner // HEAD_DIM
    conv_dim = prm['conv_w'].shape[1]
    attn_w = len(DIL_GROUPS) * HPG * HEAD_DIM
    splits = [d_inner, conv_dim, n_heads, attn_w, attn_w, attn_w, d_model, d_model]
    offs = [0]
    for n in splits:
        offs.append(offs[-1] + n)
    w_in = prm['w_in']
    seg = [w_in[:, offs[j]:offs[j + 1]].astype(BF16) for j in range(len(splits))]
    seg[2] = jnp.pad(seg[2], ((0, 0), (0, LANES - n_heads)))
    bf = lambda name: prm[name].astype(BF16)
    return dict(
        ffn1=(_row(prm['norm_ffn1']), bf('w_ffn1_gate'), bf('w_ffn1_up'), bf('w_ffn1_down'), _row(prm['norm_mix'])),
        proj_ssm=seg[0:3], proj_attn=seg[3:8],
        conv_w=prm['conv_w'].astype(F32), conv_b=_row(prm['conv_b']),
        dt_bias=_row(prm['dt_bias'], LANES), a_log=_row(prm['a_log'], LANES),
        d_skip=_row(jnp.repeat(prm['d_skip'], HEAD_DIM)), norm_ssm=_row(prm['norm_ssm']),
        out=(bf('w_o_ssm'), bf('w_o_attn'), bf('w_out'), _row(prm['norm_ffn2']), bf('w_ffn2_gate'),
             bf('w_ffn2_up'), bf('w_ffn2_down'), _row(prm['norm_ple']), bf('w_ple_gate'), bf('w_ple_proj')),
        d_inner=d_inner, conv_dim=conv_dim,
    )


def _kv_stack(k, v, gi, keep):
    b, s, _ = k.shape
    gw = HPG * HEAD_DIM
    sel = lambda t: t[:, s - keep:, gi * gw:(gi + 1) * gw].astype(F32).reshape(b, keep, HPG, HEAD_DIM)
    return jnp.stack([sel(k), sel(v)], axis=2)


def _kv_stack_wide(k, v, batch, dil, keep):
    rows = k.shape[0] // batch
    sel = lambda t: (t.reshape(batch, rows, dil * GROUP_WIDTH)[:, rows - keep // dil:, :]
                     .astype(F32).reshape(batch, keep, HPG, HEAD_DIM))
    return jnp.stack([sel(k), sel(v)], axis=2)


def _prompt_layer(x, p_emb, w, norm_final, tm):
    b, s, d = x.shape
    t = b * s
    h1, u = _ffn(x.reshape(t, d), *w['ffn1'], tm)
    z, xbc, dt = _proj(_proj_ssm_kernel, u, w['proj_ssm'], (BF16, BF16, F32), tm, "proj_ssm")
    qs, ks, vs, gs, ga = _proj_attn_wide(u, w['proj_attn'], s, tm)
    r3 = lambda a: a.reshape(b, s, a.shape[-1])
    xbc3 = r3(xbc)
    y_ssm, ssm_new = _ssd_prompt(r3(z), xbc3, r3(dt), w['conv_w'], w['conv_b'], w['dt_bias'], w['a_log'],
                                 w['d_skip'], w['norm_ssm'])
    order = sorted(range(len(DIL_GROUPS)), key=lambda gi: -DIL_GROUPS[gi][1])
    assert all(DIL_GROUPS[a][1] == MERGE_FAN * DIL_GROUPS[c][1] for a, c in zip(order, order[1:]))
    assert DIL_GROUPS[order[-1]][1] == 1
    o, lse = None, None
    for j, gi in enumerate(order):
        o, lse = _attn_prompt_group(qs[gi], ks[gi], vs[gi], b, gi, j == 0, j == len(order) - 1, o, lse)
    y = _out_stage(h1, y_ssm.reshape(t, -1), o, gs, ga, p_emb.reshape(t, -1),
                   w['out'] + (_row(norm_final),), OUT_ROW_TILE)
    kv = [_kv_stack_wide(ks[gi], vs[gi], b, dil, min(window, s)) for gi, (window, dil) in enumerate(DIL_GROUPS)]
    conv_new = xbc3[:, s - (CONV_W - 1):, :].astype(F32)
    return y.reshape(b, s, d), kv, conv_new, ssm_new


def _sample_layer(x, p_emb, w, norm_final, conv_prev, ssm_prev, caches):
    nb, s, d = x.shape
    assert s == 1
    h1, u = _ffn(x.reshape(nb, d), *w['ffn1'], nb)
    z, xbc, dt = _proj(_proj_ssm_kernel, u, w['proj_ssm'], (F32, F32, F32), nb, "proj_ssm_sample")
    q, k, v, gs, ga = _proj(functools.partial(_proj_attn_kernel, seq=1, pos0=PAST_LEN), u, w['proj_attn'],
                            (F32,) * 5, nb, "proj_attn_sample")
    xs, xdt_t, bdt, c_t, dec = _ssd_sample_prep(xbc, conv_prev.reshape(nb, -1), dt, w['conv_w'], w['conv_b'],
                                                w['dt_bias'], w['a_log'], w['d_inner'])
    dec_flat = dec[:, :ssm_prev.shape[1]].reshape(-1)
    ssm_new, y_t = _ssd_sample_state(dec_flat, ssm_prev, xdt_t, bdt, c_t)
    y_ssm = _ssd_sample_post(y_t, xs, z, w['d_skip'], w['norm_ssm'])
    by_head = lambda a: a.reshape(nb, -1, HEAD_DIM)
    attn = _attn_sample(by_head(q), by_head(k), by_head(v), caches)
    y = _out_stage(h1, y_ssm, attn.reshape(nb, -1), gs, ga, p_emb.reshape(nb, -1),
                   w['out'] + (_row(norm_final),), nb)
    r3 = lambda a: a.reshape(nb, 1, a.shape[-1])
    kv = [_kv_stack(r3(k), r3(v), gi, 1) for gi in range(len(DIL_GROUPS))]
    conv_new = jnp.concatenate([conv_prev[:, 1:], xbc[:, None, :]], axis=1)
    return y.reshape(nb, 1, d), kv, conv_new, ssm_new


PROMPT_ROW_TILE = 512
OUT_ROW_TILE = 512


def kernel(x_prompt, x_sample, cache_kv_w128, cache_kv_w512, cache_kv_w2048, state_conv, state_ssm, p_prompt, p_sample, norm_ffn1, w_ffn1_gate, w_ffn1_up, w_ffn1_down, norm_mix, w_in, conv_w, conv_b, dt_bias, a_log, d_skip, norm_ssm, w_o_ssm, w_o_attn, w_out, norm_ffn2, w_ffn2_gate, w_ffn2_up, w_ffn2_down, norm_ple, w_ple_gate, w_ple_proj, norm_final):
    depth = w_in.shape[0]
    assert depth == 1
    layer_params = dict(
        norm_ffn1=norm_ffn1, w_ffn1_gate=w_ffn1_gate, w_ffn1_up=w_ffn1_up, w_ffn1_down=w_ffn1_down,
        norm_mix=norm_mix, w_in=w_in, conv_w=conv_w, conv_b=conv_b, dt_bias=dt_bias, a_log=a_log,
        d_skip=d_skip, norm_ssm=norm_ssm, w_o_ssm=w_o_ssm, w_o_attn=w_o_attn, w_out=w_out,
        norm_ffn2=norm_ffn2, w_ffn2_gate=w_ffn2_gate, w_ffn2_up=w_ffn2_up, w_ffn2_down=w_ffn2_down,
        norm_ple=norm_ple, w_ple_gate=w_ple_gate, w_ple_proj=w_ple_proj)
    i = 0
    w = _layer_weights({name: val[i] for name, val in layer_params.items()})
    yp, kvp, convp, ssmp = _prompt_layer(x_prompt, p_prompt[i], w, norm_final, PROMPT_ROW_TILE)
    ys, kvs, convs, ssms = _sample_layer(x_sample, p_sample[i], w, norm_final, state_conv[i], state_ssm[i],
                                         (cache_kv_w128[i], cache_kv_w512[i], cache_kv_w2048[i]))
    st = lambda a: a[None]
    return (yp, ys, st(kvp[0]), st(kvp[1]), st(kvp[2]), st(convp), st(ssmp),
            st(kvs[0]), st(kvs[1]), st(kvs[2]), st(convs), st(ssms))
```

```python
import functools
import math

import jax
import jax.numpy as jnp
from jax import lax
from jax.experimental import pallas as pl
from jax.experimental.pallas import tpu as pltpu

F32 = jnp.float32
BF16 = jnp.bfloat16

EPS = 1e-6
LOG2_E = math.log2(math.e)
ROPE_THETA = 10000.0
PAST_LEN = 8192
HEAD_DIM = 64
HPG = 8
SSM_GROUPS = 4
SSM_STATE = 128
CONV_W = 4
DIL_GROUPS = ((128, 1), (512, 4), (2048, 16))
ATTN_SCALE = HEAD_DIM ** -0.5
Q_LOG2_SCALE = ATTN_SCALE * LOG2_E
NK = 128

LANES = 128
SUBLANES = 8
VMEM_LIMIT_CAP = 56 * 1024 * 1024


def _cparams(sem, vmem_bytes):
    return pltpu.CompilerParams(dimension_semantics=sem,
                                vmem_limit_bytes=int(min(VMEM_LIMIT_CAP, vmem_bytes)))


def _resident(shape):
    nd = len(shape)
    return pl.BlockSpec(shape, lambda *_: (0,) * nd, pipeline_mode=pl.Buffered(1))


def _nbytes(shape, dtype):
    return math.prod(shape) * jnp.dtype(dtype).itemsize


def _rms(x, gain):
    ms = jnp.mean(x * x, axis=-1, keepdims=True)
    return x * lax.rsqrt(ms + EPS) * gain


def _sigmoid(x):
    return 1.0 / (1.0 + jnp.exp(-x))


def _silu(x):
    half = 0.5 * x
    return half + half * jnp.tanh(half)


def _softplus(x):
    return jnp.maximum(x, 0.0) + jnp.log1p(jnp.exp(-jnp.abs(x)))


def _dot(a, b):
    return jnp.dot(a, b, preferred_element_type=F32)


def _dot_nt(a, b):
    return lax.dot_general(a, b, (((1,), (1,)), ((), ())), preferred_element_type=F32)


def _split3(x):
    hi = x.astype(BF16)
    r1 = x - hi.astype(F32)
    mid = r1.astype(BF16)
    lo = (r1 - mid.astype(F32)).astype(BF16)
    return hi, mid, lo


def _dot_sel_l(sel, x):
    hi, mid, lo = _split3(x)
    return _dot(sel, hi) + _dot(sel, mid) + _dot(sel, lo)


def _dot_sel_r(x, sel):
    hi, mid, lo = _split3(x)
    return _dot(hi, sel) + _dot(mid, sel) + _dot(lo, sel)


def _head_expand(n_heads_padded, n_heads):
    r = lax.broadcasted_iota(jnp.int32, (n_heads_padded, n_heads * HEAD_DIM), 0)
    c = lax.broadcasted_iota(jnp.int32, (n_heads_padded, n_heads * HEAD_DIM), 1)
    return (c // HEAD_DIM == r).astype(BF16)


FF_CHUNK = 256


def _swiglu_half(x, gain, wg_ref, wu_ref, wd_ref):
    u = _rms(x, gain).astype(BF16)
    d_ff = wg_ref.shape[1]
    acc = jnp.zeros(x.shape, F32)
    for c in range(d_ff // FF_CHUNK):
        sl = slice(c * FF_CHUNK, (c + 1) * FF_CHUNK)
        g = _dot(u, wg_ref[:, sl])
        up = _dot(u, wu_ref[:, sl])
        acc = acc + _dot((_silu(g) * up).astype(BF16), wd_ref[sl, :])
    return x + 0.5 * acc


def _ffn_kernel(x_ref, n1_ref, wg_ref, wu_ref, wd_ref, n2_ref, h_ref, u_ref):
    h = _swiglu_half(x_ref[...], n1_ref[...], wg_ref, wu_ref, wd_ref)
    h_ref[...] = h
    u_ref[...] = _rms(h, n2_ref[...]).astype(u_ref.dtype)


def _ffn(x, n1, wg, wu, wd, n2, tm):
    t, d = x.shape
    d_ff = wg.shape[1]
    vmem = (3 * _nbytes((d, d_ff), BF16) + 2 * 2 * _nbytes((tm, d), F32) + 2 * _nbytes((tm, d), BF16)
            + 6 * _nbytes((tm, d), F32) + 4 * _nbytes((tm, FF_CHUNK), F32))
    return pl.pallas_call(
        _ffn_kernel,
        grid=(t // tm,),
        in_specs=[pl.BlockSpec((tm, d), lambda i: (i, 0)),
                  _resident((1, d)), _resident((d, d_ff)), _resident((d, d_ff)), _resident((d_ff, d)),
                  _resident((1, d))],
        out_specs=[pl.BlockSpec((tm, d), lambda i: (i, 0)), pl.BlockSpec((tm, d), lambda i: (i, 0))],
        out_shape=[jax.ShapeDtypeStruct((t, d), F32), jax.ShapeDtypeStruct((t, d), BF16)],
        compiler_params=_cparams(("parallel",), vmem),
        name="ffn_half_step",
    )(x, n1, wg, wu, wd, n2)


def _rope_tables(tm, seq, pos0):
    row = lax.broadcasted_iota(jnp.int32, (tm, LANES), 0) + pl.program_id(0) * tm
    pos = (row % seq + pos0).astype(F32)
    lane = lax.broadcasted_iota(jnp.int32, (tm, LANES), 1)
    half = HEAD_DIM // 2
    j = (lane % half).astype(F32)
    inv_freq = jnp.exp(j * (-math.log(ROPE_THETA) / half))
    ang = pos * inv_freq
    first = (lane % HEAD_DIM) < half
    return jnp.cos(ang), jnp.where(first, -jnp.sin(ang), jnp.sin(ang)), first


def _rotary(x, cos, sin_signed, first):
    outs = []
    for c in range(x.shape[1] // LANES):
        xb = x[:, c * LANES:(c + 1) * LANES]
        other = jnp.where(first, pltpu.roll(xb, LANES - HEAD_DIM // 2, 1), pltpu.roll(xb, HEAD_DIM // 2, 1))
        outs.append(xb * cos + other * sin_signed)
    return jnp.concatenate(outs, axis=1)


def _proj_ssm_kernel(u_ref, wz_ref, wx_ref, wdt_ref, z_ref, xbc_ref, dt_ref):
    u = u_ref[...]
    z_ref[...] = _dot(u, wz_ref[...]).astype(z_ref.dtype)
    xbc_ref[...] = _dot(u, wx_ref[...]).astype(xbc_ref.dtype)
    dt_ref[...] = _dot(u, wdt_ref[...])


def _proj_attn_kernel(u_ref, wq_ref, wk_ref, wv_ref, wgs_ref, wga_ref,
                      q_ref, k_ref, v_ref, gs_ref, ga_ref, *, seq, pos0):
    u = u_ref[...]
    cos, sin_signed, first = _rope_tables(u.shape[0], seq, pos0)
    q_ref[...] = _rotary(_dot(u, wq_ref[...]), cos, sin_signed, first).astype(q_ref.dtype)
    k_ref[...] = _rotary(_dot(u, wk_ref[...]), cos, sin_signed, first).astype(k_ref.dtype)
    v_ref[...] = _dot(u, wv_ref[...]).astype(v_ref.dtype)
    gs_ref[...] = _sigmoid(_dot(u, wgs_ref[...])).astype(gs_ref.dtype)
    ga_ref[...] = _sigmoid(_dot(u, wga_ref[...])).astype(ga_ref.dtype)


GROUP_WIDTH = HPG * HEAD_DIM
_STAGED_GROUPS = [gi for gi, (_, dil) in enumerate(DIL_GROUPS) if dil > 1]


ROW_SLAB = 128


def _rope_store_wide(res, rope, scale, cos_ref, sin_ref, first, stage, out_ref, dil):
    tm = res.shape[0]
    for rb in range(tm // ROW_SLAB):
        rows = slice(rb * ROW_SLAB, (rb + 1) * ROW_SLAB)
        for j in range(GROUP_WIDTH // LANES):
            lanes = slice(j * LANES, (j + 1) * LANES)
            xb = res[rows, lanes]
            if rope:
                other = jnp.where(first, pltpu.roll(xb, LANES - HEAD_DIM // 2, 1), pltpu.roll(xb, HEAD_DIM // 2, 1))
                xb = xb * cos_ref[rows, :] + other * sin_ref[rows, :]
            if scale is not None:
                xb = xb * scale
            if dil == 1:
                out_ref[rows, lanes] = xb.astype(out_ref.dtype)
            else:
                stage[j, rows, :] = xb
    if dil > 1:
        n_rows = tm // dil
        for r in range(dil):
            for j in range(GROUP_WIDTH // LANES):
                lo = r * GROUP_WIDTH + j * LANES
                out_ref[:, lo:lo + LANES] = stage[j, pl.ds(r, n_rows, stride=dil), :].astype(out_ref.dtype)


def _rope_table_kernel(cos_ref, sin_ref, *, seq):
    cos, sin_signed, _ = _rope_tables(cos_ref.shape[0], seq, 0)
    cos_ref[...] = cos
    sin_ref[...] = sin_signed


def _rope_table(seq, tm):
    spec = pl.BlockSpec((tm, LANES), lambda i: (i, 0))
    shape = jax.ShapeDtypeStruct((seq, LANES), F32)
    return pl.pallas_call(
        functools.partial(_rope_table_kernel, seq=seq),
        grid=(seq // tm,), in_specs=[], out_specs=[spec, spec], out_shape=[shape, shape],
        compiler_params=_cparams(("parallel",), 16 * _nbytes((tm, LANES), F32)),
        name="rope_table",
    )()


def _proj_attn_wide_kernel(u_ref, cos_ref, sin_ref, wq_ref, wk_ref, wv_ref, wgs_ref, wga_ref, *refs):
    n_grp = len(DIL_GROUPS)
    q_refs, k_refs, v_refs = refs[0:n_grp], refs[n_grp:2 * n_grp], refs[2 * n_grp:3 * n_grp]
    gs_ref, ga_ref, stage = refs[3 * n_grp:]
    u = u_ref[...]
    first = lax.broadcasted_iota(jnp.int32, (ROW_SLAB, LANES), 1) % HEAD_DIM < HEAD_DIM // 2
    for ti, (w_ref, out_refs, rope, scale) in enumerate(((wq_ref, q_refs, True, Q_LOG2_SCALE),
                                                        (wk_ref, k_refs, True, None),
                                                        (wv_ref, v_refs, False, None))):
        for gi, (_, dil) in enumerate(DIL_GROUPS):
            res = _dot(u, w_ref[:, gi * GROUP_WIDTH:(gi + 1) * GROUP_WIDTH])
            slot = None if dil == 1 else stage.at[ti * len(_STAGED_GROUPS) + _STAGED_GROUPS.index(gi)]
            _rope_store_wide(res, rope, scale, cos_ref, sin_ref, first, slot, out_refs[gi], dil)
    gs_ref[...] = _sigmoid(_dot(u, wgs_ref[...])).astype(gs_ref.dtype)
    ga_ref[...] = _sigmoid(_dot(u, wga_ref[...])).astype(ga_ref.dtype)


def _proj_attn_wide(u, weights, seq, tm):
    t, d = u.shape
    d_model = weights[3].shape[1]
    row = lambda i: (i, 0)
    qkv_specs = [pl.BlockSpec((tm // dil, dil * GROUP_WIDTH), row) for _, dil in DIL_GROUPS] * 3
    qkv_shapes = [jax.ShapeDtypeStruct((t // dil, dil * GROUP_WIDTH), BF16) for _, dil in DIL_GROUPS] * 3
    n_stage = 3 * len(_STAGED_GROUPS)
    vmem = (sum(_nbytes(w.shape, BF16) for w in weights) + 2 * _nbytes((tm, d), BF16)
            + 2 * _nbytes((tm, 3 * len(DIL_GROUPS) * GROUP_WIDTH + 2 * d_model), BF16)
            + 4 * _nbytes((tm, LANES), F32)
            + n_stage * _nbytes((tm, GROUP_WIDTH), F32) + 6 * _nbytes((tm, weights[0].shape[1]), F32))
    cos, sin_signed = _rope_table(seq, tm)
    tiles_per_seq = seq // tm
    table = pl.BlockSpec((tm, LANES), lambda i: (i % tiles_per_seq, 0))
    outs = pl.pallas_call(
        _proj_attn_wide_kernel,
        grid=(t // tm,),
        in_specs=[pl.BlockSpec((tm, d), row), table, table] + [_resident(w.shape) for w in weights],
        out_specs=qkv_specs + [pl.BlockSpec((tm, d_model), row)] * 2,
        out_shape=qkv_shapes + [jax.ShapeDtypeStruct((t, d_model), BF16)] * 2,
        scratch_shapes=[pltpu.VMEM((n_stage, GROUP_WIDTH // LANES, tm, LANES), F32)],
        compiler_params=_cparams(("parallel",), vmem),
        name="proj_attn",
    )(u, cos, sin_signed, *weights)
    n_grp = len(DIL_GROUPS)
    return outs[0:n_grp], outs[n_grp:2 * n_grp], outs[2 * n_grp:3 * n_grp], outs[3 * n_grp], outs[3 * n_grp + 1]


def _proj(kernel, u, weights, out_dtypes, tm, name):
    t, d = u.shape
    row = lambda i: (i, 0)
    widths = [w.shape[1] for w in weights]
    vmem = (sum(_nbytes(w.shape, BF16) for w in weights) + 2 * _nbytes((tm, d), BF16)
            + sum(2 * _nbytes((tm, n), dt) for n, dt in zip(widths, out_dtypes))
            + 4 * _nbytes((tm, max(widths)), F32))
    return pl.pallas_call(
        kernel,
        grid=(t // tm,),
        in_specs=[pl.BlockSpec((tm, d), row)] + [_resident(w.shape) for w in weights],
        out_specs=[pl.BlockSpec((tm, n), row) for n in widths],
        out_shape=[jax.ShapeDtypeStruct((t, n), dt) for n, dt in zip(widths, out_dtypes)],
        compiler_params=_cparams(("parallel",), vmem),
        name=name,
    )(u, *weights)


SSD_CHUNK = 128
SSD_CHUNKS_PER_STEP = 4


def _ssd_gate_norm(y, z, gain):
    return _rms(y * _silu(z), gain)


def _ssd_chunk(x_prev, x_cur, z, dt_raw, states, cw_ref, cb_ref, dtb_ref, alog_ref, dskip_ref, ng_ref,
               shift_ref, expand_ref):
    L = SSD_CHUNK
    d_inner = z.shape[-1]
    gw = HPG * HEAD_DIM
    gn = SSM_GROUPS * SSM_STATE

    both = jnp.concatenate([x_prev, x_cur], axis=0)
    conv = cb_ref[...] + x_cur.astype(F32) * cw_ref[CONV_W - 1:CONV_W, :]
    shifted = _dot(shift_ref[...], both)
    for back in range(1, CONV_W):
        tap = CONV_W - 1 - back
        conv = conv + shifted[(back - 1) * L:back * L, :] * cw_ref[tap:tap + 1, :]
    xa = _silu(conv)
    xs = xa[:, :d_inner]
    bm = xa[:, d_inner:d_inner + gn]
    cm = xa[:, d_inner + gn:]

    dt = _softplus(dt_raw + dtb_ref[...])
    a = -jnp.exp(alog_ref[...])
    row = lax.broadcasted_iota(jnp.int32, (L, L), 0)
    col = lax.broadcasted_iota(jnp.int32, (L, L), 1)
    causal = row >= col
    acs = _dot_sel_l(causal.astype(BF16), dt * a)
    acs_log2 = acs * LOG2_E
    acs_log2_t = acs_log2.T
    dt_t = dt.T
    acs_last = acs[L - 1:L, :]

    expand = expand_ref[...]
    per_head = jnp.concatenate([jnp.exp(acs), jnp.exp(acs_last - acs) * dt], axis=0)
    per_lane = _dot(per_head.astype(BF16), expand)
    e_acs, w_end = per_lane[:L], per_lane[L:]
    dec_state = _dot_sel_r(jnp.broadcast_to(jnp.exp(acs_last), (SUBLANES, LANES)), expand)[0:1, :]

    xs_b = xs.astype(BF16)
    xw = (xs * w_end).astype(BF16)
    lane_first = lax.broadcasted_iota(jnp.int32, (L, LANES), 1) < HEAD_DIM
    y_groups, new_states = [], []
    for g in range(SSM_GROUPS):
        gsl = slice(g * gw, (g + 1) * gw)
        b_g = bm[:, g * SSM_STATE:(g + 1) * SSM_STATE]
        c_g = cm[:, g * SSM_STATE:(g + 1) * SSM_STATE].astype(BF16)
        cb = _dot_nt(c_g, b_g.astype(BF16))
        st_g = states[g]
        y_inter = _dot(c_g, st_g.astype(BF16))
        pairs = []
        for pr in range(HPG // 2):
            w_pair = []
            for sub in range(2):
                h = g * HPG + pr * 2 + sub
                seg = acs_log2[:, h:h + 1] - acs_log2_t[h:h + 1, :]
                w = cb * jnp.exp2(jnp.where(causal, seg, -jnp.inf)) * dt_t[h:h + 1, :]
                w_pair.append(w.astype(BF16))
            lo = g * gw + pr * LANES
            res = _dot(jnp.concatenate(w_pair, axis=0), xs_b[:, lo:lo + LANES])
            pairs.append(jnp.where(lane_first, res[:L], res[L:]))
        y_groups.append(jnp.concatenate(pairs, axis=1) + y_inter * e_acs[:, gsl])
        new_states.append(dec_state[:, gsl] * st_g + _dot(b_g.T.astype(BF16), xw[:, gsl]))
    y = jnp.concatenate(y_groups, axis=1) + dskip_ref[...] * xs
    return _ssd_gate_norm(y, z.astype(F32), ng_ref[...]), new_states


def _ssd_prompt_kernel(z_ref, xprev_ref, xbc_ref, dt_ref, cw_ref, cb_ref, dtb_ref, alog_ref, dskip_ref, ng_ref,
                       shift_ref, expand_ref, y_ref, st_ref, st_t):
    L = SSD_CHUNK
    c = pl.program_id(1)

    @pl.when(c == 0)
    def _():
        st_t[...] = jnp.zeros(st_t.shape, F32)

    states = [st_t[g] for g in range(SSM_GROUPS)]
    x_prev = jnp.where(c > 0, xprev_ref[0], jnp.zeros(xprev_ref.shape[1:], xprev_ref.dtype))
    for j in range(xbc_ref.shape[1] // L):
        rows = slice(j * L, (j + 1) * L)
        x_cur = xbc_ref[0, rows, :]
        y, states = _ssd_chunk(x_prev, x_cur, z_ref[0, rows, :], dt_ref[0, rows, :], states,
                               cw_ref, cb_ref, dtb_ref, alog_ref, dskip_ref, ng_ref, shift_ref, expand_ref)
        y_ref[0, rows, :] = y.astype(y_ref.dtype)
        x_prev = x_cur
    for g in range(SSM_GROUPS):
        st_t[g] = states[g]

    @pl.when(c == pl.num_programs(1) - 1)
    def _():
        for g in range(SSM_GROUPS):
            st_ref[0, g * HPG:(g + 1) * HPG] = states[g].T.reshape(HPG, HEAD_DIM, SSM_STATE)


def _ssd_prompt(z, xbc, dt, cw, cb, dtb, alog, dskip, ng):
    b, s, d_inner = z.shape
    conv_dim = xbc.shape[-1]
    n_heads = d_inner // HEAD_DIM
    L = SSD_CHUNK
    per_step = math.gcd(s // L, SSD_CHUNKS_PER_STEP)
    blk = lambda n: pl.BlockSpec((1, per_step * L, n), lambda i, c: (i, c, 0))
    r = jnp.arange(L)[:, None]
    cidx = jnp.arange(2 * L)[None, :]
    shift = jnp.concatenate([(cidx == r + L - k) for k in range(1, CONV_W)], axis=0).astype(BF16)
    heads = jnp.arange(LANES)[:, None]
    expand = (jnp.arange(d_inner)[None, :] // HEAD_DIM == heads).astype(BF16)
    prev_blk = pl.BlockSpec((1, L, conv_dim), lambda i, c: (i, jnp.maximum(c * per_step - 1, 0), 0))
    vmem = (2 * per_step * (_nbytes((L, d_inner), z.dtype) * 2 + _nbytes((L, conv_dim), xbc.dtype)
                            + _nbytes((L, LANES), F32))
            + 2 * _nbytes((L, conv_dim), xbc.dtype)
            + 2 * _nbytes((n_heads, HEAD_DIM, SSM_STATE), F32)
            + _nbytes((SSM_GROUPS, SSM_STATE, HPG * HEAD_DIM), F32)
            + _nbytes(shift.shape, BF16) + _nbytes(expand.shape, BF16)
            + 16 * per_step * _nbytes((L, conv_dim), F32))
    return pl.pallas_call(
        _ssd_prompt_kernel,
        grid=(b, s // (per_step * L)),
        in_specs=[blk(d_inner), prev_blk, blk(conv_dim), blk(LANES),
                  _resident(cw.shape), _resident(cb.shape), _resident(dtb.shape), _resident(alog.shape),
                  _resident(dskip.shape), _resident(ng.shape), _resident(shift.shape), _resident(expand.shape)],
        out_specs=[blk(d_inner),
                   pl.BlockSpec((1, n_heads, HEAD_DIM, SSM_STATE), lambda i, c: (i, 0, 0, 0))],
        out_shape=[jax.ShapeDtypeStruct((b, s, d_inner), BF16),
                   jax.ShapeDtypeStruct((b, n_heads, HEAD_DIM, SSM_STATE), F32)],
        scratch_shapes=[pltpu.VMEM((SSM_GROUPS, SSM_STATE, HPG * HEAD_DIM), F32)],
        compiler_params=_cparams(("parallel", "arbitrary"), vmem),
        name="ssd_prompt_scan",
    )(z, xbc, xbc, dt, cw, cb, dtb, alog, dskip, ng, shift, expand)


MERGE_FAN = 4
ATTN_BLOCKS_PER_STEP = 8


def _attn_prompt_kernel(*refs, first, last):
    q_ref, kp_ref, kc_ref, vp_ref, vc_ref = refs[:5]
    refs = refs[5:]
    if not first:
        oin_refs, lin_refs = refs[:MERGE_FAN], refs[MERGE_FAN:2 * MERGE_FAN]
        refs = refs[2 * MERGE_FAN:]
    out_refs = refs
    o_ref = out_refs[0]
    n = pl.program_id(2)
    n_blocks = q_ref.shape[0] // NK
    qi = lax.broadcasted_iota(jnp.int32, (NK, 2 * NK), 0)
    kj = lax.broadcasted_iota(jnp.int32, (NK, 2 * NK), 1)
    dist = qi + NK - kj
    band = (dist >= 0) & (dist <= NK)
    lane = lax.broadcasted_iota(jnp.int32, (NK, LANES), 1)
    lane_first = lane < HEAD_DIM
    if not first:
        rows = NK // MERGE_FAN
        dst = lax.broadcasted_iota(jnp.int32, (NK, NK), 0)
        src = lax.broadcasted_iota(jnp.int32, (NK, NK), 1)
        perm = (src == (dst % MERGE_FAN) * rows + dst // MERGE_FAN).astype(BF16)
        expand = _head_expand(LANES, HPG)
    for j in range(n_blocks):
        own = slice(j * NK, (j + 1) * NK)
        q = q_ref[own, :]
        if j == 0:
            k_before, v_before = kp_ref[...], vp_ref[...]
            mask = band & ((kj >= NK) | (n > 0))
        else:
            before = slice((j - 1) * NK, j * NK)
            k_before, v_before = kc_ref[before, :], vc_ref[before, :]
            mask = band
        kk = jnp.concatenate([k_before, kc_ref[own, :]], axis=0)
        vv = jnp.concatenate([v_before, vc_ref[own, :]], axis=0)
        lse_out = jnp.zeros((NK, LANES), F32)
        o_parts = []
        for pr in range(HPG // 2):
            sl = slice(pr * LANES, (pr + 1) * LANES)
            q_pair, k_pair, v_pair = q[:, sl], kk[:, sl], vv[:, sl]
            res = []
            for sub in range(2):
                h = pr * 2 + sub
                qm = jnp.where(lane_first if sub == 0 else ~lane_first, q_pair, jnp.zeros_like(q_pair))
                s = jnp.where(mask, _dot_nt(qm, k_pair), -jnp.inf)
                m = jnp.max(s, axis=-1, keepdims=True)
                p = jnp.exp2(s - m)
                l = jnp.sum(p, axis=-1, keepdims=True)
                res.append(_dot(p.astype(BF16), v_pair) / l)
                lse_out = jnp.where(lane == h, m + jnp.log2(l), lse_out)
            o_parts.append(jnp.where(lane_first, res[0], res[1]))
        o = jnp.concatenate(o_parts, axis=1)
        if not first:
            part = slice(j * rows, (j + 1) * rows)
            o_prev = _dot(perm, jnp.concatenate([r[part, :] for r in oin_refs], axis=0))
            lse_prev = _dot_sel_l(perm, jnp.concatenate([r[part, :] for r in lin_refs], axis=0))
            lse_new = jnp.maximum(lse_out, lse_prev) + jnp.log2(1.0 + jnp.exp2(-jnp.abs(lse_out - lse_prev)))
            w_prev = _dot_sel_r(jnp.exp2(lse_prev - lse_new), expand)
            w_cur = _dot_sel_r(jnp.exp2(lse_out - lse_new), expand)
            o = o_prev * w_prev + o * w_cur
            lse_out = lse_new
        o_ref[own, :] = o.astype(o_ref.dtype)
        if not last:
            out_refs[1][own, :] = lse_out


def _attn_prompt_group(q, k, v, batch, gi, first, last, o_in, lse_in):
    window, dil = DIL_GROUPS[gi]
    td = q.shape[0]
    sd = td // batch
    assert window // dil == NK and sd % NK == 0 and q.shape[1] == dil * GROUP_WIDTH
    nb = sd // NK
    g = math.gcd(nb, ATTN_BLOCKS_PER_STEP)
    ns = nb // g
    blk = lambda width: pl.BlockSpec((g * NK, width), lambda i, r, n: (i * ns + n, r))
    prev = pl.BlockSpec((NK, GROUP_WIDTH), lambda i, r, n: (i * nb + jnp.maximum(n * g - 1, 0), r))
    args, in_specs = [q, k, k, v, v], [blk(GROUP_WIDTH), prev, blk(GROUP_WIDTH), prev, blk(GROUP_WIDTH)]
    if not first:
        assert o_in.shape == (td // MERGE_FAN, MERGE_FAN * dil * GROUP_WIDTH)
        piece = lambda width, c: pl.BlockSpec((g * NK // MERGE_FAN, width),
                                              lambda i, r, n: (i * ns + n, dil * c + r))
        args += [o_in] * MERGE_FAN + [lse_in] * MERGE_FAN
        in_specs += [piece(GROUP_WIDTH, c) for c in range(MERGE_FAN)] + [piece(LANES, c) for c in range(MERGE_FAN)]
    out_specs = [blk(GROUP_WIDTH)]
    out_shape = [jax.ShapeDtypeStruct((td, dil * GROUP_WIDTH), BF16)]
    if not last:
        out_specs.append(blk(LANES))
        out_shape.append(jax.ShapeDtypeStruct((td, dil * LANES), F32))
    vmem = (2 * (6 * g + 2) * _nbytes((NK, GROUP_WIDTH), BF16) + 6 * g * _nbytes((NK, LANES), F32)
            + 24 * _nbytes((NK, 2 * NK), F32) + 6 * g * _nbytes((NK, GROUP_WIDTH), F32))
    outs = pl.pallas_call(
        functools.partial(_attn_prompt_kernel, first=first, last=last),
        grid=(batch, dil, ns),
        in_specs=in_specs, out_specs=out_specs, out_shape=out_shape,
        compiler_params=_cparams(("parallel", "parallel", "parallel"), vmem),
        name=f"attn_prompt_w{window}",
    )(*args)
    return outs[0], (None if last else outs[1])


def _out_kernel(h_ref, y_ref, at_ref, gs_ref, ga_ref, p_ref,
                wos_ref, woa_ref, wo_ref, n2_ref, wg_ref, wu_ref, wd_ref, np_ref, wpg_ref, wpp_ref, nf_ref,
                o_ref):
    merged = (gs_ref[...].astype(F32) * _dot(y_ref[...].astype(BF16), wos_ref[...])
              + ga_ref[...].astype(F32) * _dot(at_ref[...].astype(BF16), woa_ref[...]))
    h = h_ref[...] + _dot(merged.astype(BF16), wo_ref[...])
    h = _swiglu_half(h, n2_ref[...], wg_ref, wu_ref, wd_ref)
    gate = _sigmoid(_dot(_rms(h, np_ref[...]).astype(BF16), wpg_ref[...]))
    h = h + gate * _dot(p_ref[...].astype(BF16), wpp_ref[...])
    o_ref[...] = _rms(h, nf_ref[...])


def _out_stage(h, y, attn, gs, ga, p, weights, tm):
    t, d = h.shape
    row = lambda i: (i, 0)
    acts = [h, y, attn, gs, ga, p]
    vmem = (sum(_nbytes(w.shape, w.dtype) for w in weights)
            + sum(2 * _nbytes((tm, a.shape[1]), a.dtype) for a in acts) + 2 * _nbytes((tm, d), F32)
            + 8 * _nbytes((tm, d), F32) + 4 * _nbytes((tm, FF_CHUNK), F32))
    return pl.pallas_call(
        _out_kernel,
        grid=(t // tm,),
        in_specs=[pl.BlockSpec((tm, a.shape[1]), row) for a in acts] + [_resident(w.shape) for w in weights],
        out_specs=pl.BlockSpec((tm, d), row),
        out_shape=jax.ShapeDtypeStruct((t, d), F32),
        compiler_params=_cparams(("parallel",), vmem),
        name="output_stage",
    )(*acts, *weights)


def _ssd_sample_prep_kernel(xn_ref, sc_ref, dtr_ref, cw_ref, cb_ref, dtb_ref, alog_ref,
                            xs_ref, xdt_t_ref, bdt_ref, c_t_ref, dec_ref):
    conv_dim = xn_ref.shape[1]
    d_inner = xs_ref.shape[1]
    gn = SSM_GROUPS * SSM_STATE
    conv = cb_ref[...] + xn_ref[...] * cw_ref[CONV_W - 1:CONV_W, :]
    for tap in range(CONV_W - 1):
        conv = conv + sc_ref[:, tap * conv_dim:(tap + 1) * conv_dim] * cw_ref[tap:tap + 1, :]
    xa = _silu(conv)
    xs = xa[:, :d_inner]
    xs_ref[...] = xs
    dt = _softplus(dtr_ref[...] + dtb_ref[...])
    dec_ref[...] = jnp.exp(dt * -jnp.exp(alog_ref[...]))
    n_heads = d_inner // HEAD_DIM
    dt_x = _dot_sel_r(dt, _head_expand(LANES, n_heads))
    xdt_t_ref[...] = (xs * dt_x).T.astype(xdt_t_ref.dtype)
    bdt_ref[...] = xa[:, d_inner:d_inner + gn]
    for g in range(SSM_GROUPS):
        lo = d_inner + gn + g * SSM_STATE
        c_t_ref[g] = xa[:, lo:lo + SSM_STATE].T.astype(c_t_ref.dtype)


def _ssd_sample_prep(xn, sc, dtr, cw, cb, dtb, alog, d_inner):
    nb, conv_dim = xn.shape
    args = [xn, sc, dtr, cw, cb, dtb, alog]
    out_shape = [jax.ShapeDtypeStruct((nb, d_inner), F32),
                 jax.ShapeDtypeStruct((d_inner, nb), BF16),
                 jax.ShapeDtypeStruct((nb, SSM_GROUPS * SSM_STATE), F32),
                 jax.ShapeDtypeStruct((SSM_GROUPS, SSM_STATE, nb), BF16),
                 jax.ShapeDtypeStruct((nb, LANES), F32)]
    vmem = 2 * sum(_nbytes(a.shape, a.dtype) for a in args) + 2 * sum(_nbytes(o.shape, o.dtype) for o in out_shape) \
        + 8 * _nbytes((nb, conv_dim), F32)
    return pl.pallas_call(
        _ssd_sample_prep_kernel,
        grid=(1,),
        in_specs=[_resident(a.shape) for a in args],
        out_specs=[pl.BlockSpec(o.shape, functools.partial(lambda nd, i: (0,) * nd, len(o.shape))) for o in out_shape],
        out_shape=out_shape,
        compiler_params=_cparams(("arbitrary",), vmem),
        name="ssd_sample_prep",
    )(*args)


SAMPLE_STATE_TILE = 4


def _ssd_sample_state_kernel(dec_ref, st_ref, xdt_t_ref, bdt_ref, c_t_ref, sto_ref, y_t_ref):
    i = pl.program_id(0)
    nb = bdt_ref.shape[0]
    n_heads = st_ref.shape[1]
    gw = HPG * HEAD_DIM

    @pl.when(i == 0)
    def _():
        y_t_ref[...] = jnp.zeros(y_t_ref.shape, F32)

    row = lax.broadcasted_iota(jnp.int32, (nb, SSM_STATE), 0)
    lane = lax.broadcasted_iota(jnp.int32, (gw, nb), 1)
    for j in range(SAMPLE_STATE_TILE):
        b = i * SAMPLE_STATE_TILE + j
        for g in range(SSM_GROUPS):
            b_row = jnp.where(row == b, bdt_ref[:, g * SSM_STATE:(g + 1) * SSM_STATE], 0.0).astype(BF16)
            upd = _dot(xdt_t_ref[g * gw:(g + 1) * gw, :], b_row)
            new = []
            for hg in range(HPG):
                h = g * HPG + hg
                new_h = st_ref[j, h] * dec_ref[b * n_heads + h] + upd[hg * HEAD_DIM:(hg + 1) * HEAD_DIM, :]
                sto_ref[j, h] = new_h
                new.append(new_h)
            yg = _dot(jnp.concatenate(new, axis=0).astype(BF16), c_t_ref[g])
            y_t_ref[g * gw:(g + 1) * gw, :] += jnp.where(lane == b, yg, 0.0)


def _ssd_sample_state(dec, st, xdt_t, bdt, c_t):
    nb, n_heads = st.shape[0], st.shape[1]
    d_inner = n_heads * HEAD_DIM
    bt = SAMPLE_STATE_TILE
    st_spec = pl.BlockSpec((bt, n_heads, HEAD_DIM, SSM_STATE), lambda i: (i, 0, 0, 0))
    vmem = (4 * _nbytes((bt, n_heads, HEAD_DIM, SSM_STATE), F32) + _nbytes(xdt_t.shape, BF16)
            + _nbytes(bdt.shape, F32) + _nbytes(c_t.shape, BF16) + 2 * _nbytes((d_inner, nb), F32)
            + 16 * _nbytes((HPG * HEAD_DIM, SSM_STATE), F32))
    return pl.pallas_call(
        _ssd_sample_state_kernel,
        grid=(nb // bt,),
        in_specs=[pl.BlockSpec(memory_space=pltpu.SMEM), st_spec,
                  _resident(xdt_t.shape), _resident(bdt.shape), _resident(c_t.shape)],
        out_specs=[st_spec, pl.BlockSpec((d_inner, nb), lambda i: (0, 0))],
        out_shape=[jax.ShapeDtypeStruct(st.shape, F32), jax.ShapeDtypeStruct((d_inner, nb), F32)],
        compiler_params=_cparams(("arbitrary",), vmem),
        name="ssd_sample_state",
    )(dec, st, xdt_t, bdt, c_t)


def _ssd_sample_post_kernel(y_t_ref, xs_ref, z_ref, dskip_ref, ng_ref, y_ref):
    y = y_t_ref[...].T + dskip_ref[...] * xs_ref[...]
    y_ref[...] = _ssd_gate_norm(y, z_ref[...], ng_ref[...]).astype(y_ref.dtype)


def _ssd_sample_post(y_t, xs, z, dskip, ng):
    args = [y_t, xs, z, dskip, ng]
    vmem = 2 * sum(_nbytes(a.shape, a.dtype) for a in args) + 8 * _nbytes(xs.shape, F32)
    return pl.pallas_call(
        _ssd_sample_post_kernel,
        grid=(1,),
        in_specs=[_resident(a.shape) for a in args],
        out_specs=pl.BlockSpec(xs.shape, lambda i: (0, 0)),
        out_shape=jax.ShapeDtypeStruct(xs.shape, F32),
        compiler_params=_cparams(("arbitrary",), vmem),
        name="ssd_sample_post",
    )(*args)


def _attn_sample_kernel(q_ref, qt_ref, k_ref, v_ref, c0_ref, c1_ref, c2_ref, o_ref):
    q_rows = q_ref[0] * ATTN_SCALE
    q_cols = qt_ref[0] * ATTN_SCALE
    s_new_all = jnp.sum(k_ref[0] * q_rows, axis=-1, keepdims=True)
    scores = []
    for gi, (c_ref, (_, dil)) in enumerate(zip((c0_ref, c1_ref, c2_ref), DIL_GROUPS)):
        rows = []
        for h in range(HPG):
            col = gi * HPG + h
            rows.append(jnp.sum(c_ref[0, 0, h] * q_cols[:, col:col + 1], axis=0, keepdims=True))
        s = jnp.concatenate(rows, axis=0)
        pos = lax.broadcasted_iota(jnp.int32, s.shape, 1)
        scores.append((jnp.where(pos % dil == 0, s, -jnp.inf), s_new_all[gi * HPG:(gi + 1) * HPG]))
    m = functools.reduce(jnp.maximum, [jnp.maximum(jnp.max(s, axis=1, keepdims=True), sn) for s, sn in scores])
    den = jnp.zeros((HPG, 1), F32)
    o_new = jnp.zeros((HPG, HEAD_DIM), F32)
    o_cache_t = jnp.zeros((HEAD_DIM, LANES), F32)
    lane = lax.broadcasted_iota(jnp.int32, (HEAD_DIM, LANES), 1)
    for gi, ((s, sn), c_ref) in enumerate(zip(scores, (c0_ref, c1_ref, c2_ref))):
        p = jnp.exp(s - m)
        p_n = jnp.exp(sn - m)
        den = den + jnp.sum(p, axis=1, keepdims=True) + p_n
        o_new = o_new + p_n * v_ref[0, gi * HPG:(gi + 1) * HPG, :]
        for h in range(HPG):
            contrib = jnp.sum(c_ref[0, 1, h] * p[h:h + 1, :], axis=1, keepdims=True)
            o_cache_t = jnp.where(lane == h, o_cache_t + contrib, o_cache_t)
    o_cache = jnp.concatenate([o_cache_t, jnp.zeros_like(o_cache_t)], axis=0).T
    o_ref[0] = (o_cache[:HPG, :HEAD_DIM] + o_new) / den


def _attn_sample(q, k, v, caches):
    nb, n_heads, _ = q.shape
    views, specs = [], []
    for (window, dil), c in zip(DIL_GROUPS, caches):
        assert c.shape[1] == window and window // dil == NK
        views.append(jnp.transpose(c, (0, 2, 3, 4, 1)))
        specs.append(pl.BlockSpec((1, 2, HPG, HEAD_DIM, window), lambda i: (i, 0, 0, 0, 0)))
    vec = pl.BlockSpec((1, n_heads, HEAD_DIM), lambda i: (i, 0, 0))
    vec_t = pl.BlockSpec((1, HEAD_DIM, n_heads), lambda i: (i, 0, 0))
    cache_bytes = sum(_nbytes((2, HPG, HEAD_DIM, window), F32) for window, _ in DIL_GROUPS)
    vmem = 2 * cache_bytes + 6 * _nbytes((HEAD_DIM, DIL_GROUPS[-1][0]), F32) + 8 * _nbytes((LANES, LANES), F32)
    return pl.pallas_call(
        _attn_sample_kernel,
        grid=(nb,),
        in_specs=[vec, vec_t, vec, vec] + specs,
        out_specs=pl.BlockSpec((1, HPG, HEAD_DIM), lambda i: (i, 0, 0)),
        out_shape=jax.ShapeDtypeStruct((nb, HPG, HEAD_DIM), F32),
        compiler_params=_cparams(("parallel",), vmem),
        name="attn_sample",
    )(q, jnp.swapaxes(q, 1, 2), k, v, *views)


def _row(v, width=None):
    v = v.astype(F32).reshape(1, -1)
    if width is not None and v.shape[1] < width:
        v = jnp.pad(v, ((0, 0), (0, width - v.shape[1])))
    return v


def _layer_weights(prm):
    d_model = prm['w_in'].shape[0]
    d_inner = prm['norm_ssm'].shape[0]
    n_heads = d_inner // HEAD_DIM
    conv_dim = prm['conv_w'].shape[1]
    attn_w = len(DIL_GROUPS) * HPG * HEAD_DIM
    splits = [d_inner, conv_dim, n_heads, attn_w, attn_w, attn_w, d_model, d_model]
    offs = [0]
    for n in splits:
        offs.append(offs[-1] + n)
    w_in = prm['w_in']
    seg = [w_in[:, offs[j]:offs[j + 1]].astype(BF16) for j in range(len(splits))]
    seg[2] = jnp.pad(seg[2], ((0, 0), (0, LANES - n_heads)))
    bf = lambda name: prm[name].astype(BF16)
    return dict(
        ffn1=(_row(prm['norm_ffn1']), bf('w_ffn1_gate'), bf('w_ffn1_up'), bf('w_ffn1_down'), _row(prm['norm_mix'])),
        proj_ssm=seg[0:3], proj_attn=seg[3:8],
        conv_w=prm['conv_w'].astype(F32), conv_b=_row(prm['conv_b']),
        dt_bias=_row(prm['dt_bias'], LANES), a_log=_row(prm['a_log'], LANES),
        d_skip=_row(jnp.repeat(prm['d_skip'], HEAD_DIM)), norm_ssm=_row(prm['norm_ssm']),
        out=(bf('w_o_ssm'), bf('w_o_attn'), bf('w_out'), _row(prm['norm_ffn2']), bf('w_ffn2_gate'),
             bf('w_ffn2_up'), bf('w_ffn2_down'), _row(prm['norm_ple']), bf('w_ple_gate'), bf('w_ple_proj')),
        d_inner=d_inner, conv_dim=conv_dim,
    )


def _kv_stack(k, v, gi, keep):
    b, s, _ = k.shape
    gw = HPG * HEAD_DIM
    sel = lambda t: t[:, s - keep:, gi * gw:(gi + 1) * gw].astype(F32).reshape(b, keep, HPG, HEAD_DIM)
    return jnp.stack([sel(k), sel(v)], axis=2)


def _kv_stack_wide(k, v, batch, dil, keep):
    rows = k.shape[0] // batch
    sel = lambda t: (t.reshape(batch, rows, dil * GROUP_WIDTH)[:, rows - keep // dil:, :]
                     .astype(F32).reshape(batch, keep, HPG, HEAD_DIM))
    return jnp.stack([sel(k), sel(v)], axis=2)


def _prompt_layer(x, p_emb, w, norm_final, tm):
    b, s, d = x.shape
    t = b * s
    h1, u = _ffn(x.reshape(t, d), *w['ffn1'], tm)
    z, xbc, dt = _proj(_proj_ssm_kernel, u, w['proj_ssm'], (BF16, BF16, F32), tm, "proj_ssm")
    qs, ks, vs, gs, ga = _proj_attn_wide(u, w['proj_attn'], s, tm)
    r3 = lambda a: a.reshape(b, s, a.shape[-1])
    xbc3 = r3(xbc)
    y_ssm, ssm_new = _ssd_prompt(r3(z), xbc3, r3(dt), w['conv_w'], w['conv_b'], w['dt_bias'], w['a_log'],
                                 w['d_skip'], w['norm_ssm'])
    order = sorted(range(len(DIL_GROUPS)), key=lambda gi: -DIL_GROUPS[gi][1])
    assert all(DIL_GROUPS[a][1] == MERGE_FAN * DIL_GROUPS[c][1] for a, c in zip(order, order[1:]))
    assert DIL_GROUPS[order[-1]][1] == 1
    o, lse = None, None
    for j, gi in enumerate(order):
        o, lse = _attn_prompt_group(qs[gi], ks[gi], vs[gi], b, gi, j == 0, j == len(order) - 1, o, lse)
    y = _out_stage(h1, y_ssm.reshape(t, -1), o, gs, ga, p_emb.reshape(t, -1),
                   w['out'] + (_row(norm_final),), OUT_ROW_TILE)
    kv = [_kv_stack_wide(ks[gi], vs[gi], b, dil, min(window, s)) for gi, (window, dil) in enumerate(DIL_GROUPS)]
    conv_new = xbc3[:, s - (CONV_W - 1):, :].astype(F32)
    return y.reshape(b, s, d), kv, conv_new, ssm_new


def _sample_layer(x, p_emb, w, norm_final, conv_prev, ssm_prev, caches):
    nb, s, d = x.shape
    assert s == 1
    h1, u = _ffn(x.reshape(nb, d), *w['ffn1'], nb)
    z, xbc, dt = _proj(_proj_ssm_kernel, u, w['proj_ssm'], (F32, F32, F32), nb, "proj_ssm_sample")
    q, k, v, gs, ga = _proj(functools.partial(_proj_attn_kernel, seq=1, pos0=PAST_LEN), u, w['proj_attn'],
                            (F32,) * 5, nb, "proj_attn_sample")
    xs, xdt_t, bdt, c_t, dec = _ssd_sample_prep(xbc, conv_prev.reshape(nb, -1), dt, w['conv_w'], w['conv_b'],
                                                w['dt_bias'], w['a_log'], w['d_inner'])
    dec_flat = dec[:, :ssm_prev.shape[1]].reshape(-1)
    ssm_new, y_t = _ssd_sample_state(dec_flat, ssm_prev, xdt_t, bdt, c_t)
    y_ssm = _ssd_sample_post(y_t, xs, z, w['d_skip'], w['norm_ssm'])
    by_head = lambda a: a.reshape(nb, -1, HEAD_DIM)
    attn = _attn_sample(by_head(q), by_head(k), by_head(v), caches)
    y = _out_stage(h1, y_ssm, attn.reshape(nb, -1), gs, ga, p_emb.reshape(nb, -1),
                   w['out'] + (_row(norm_final),), nb)
    r3 = lambda a: a.reshape(nb, 1, a.shape[-1])
    kv = [_kv_stack(r3(k), r3(v), gi, 1) for gi in range(len(DIL_GROUPS))]
    conv_new = jnp.concatenate([conv_prev[:, 1:], xbc[:, None, :]], axis=1)
    return y.reshape(nb, 1, d), kv, conv_new, ssm_new


PROMPT_ROW_TILE = 512
OUT_ROW_TILE = 512


def kernel(x_prompt, x_sample, cache_kv_w128, cache_kv_w512, cache_kv_w2048, state_conv, state_ssm, p_prompt, p_sample, norm_ffn1, w_ffn1_gate, w_ffn1_up, w_ffn1_down, norm_mix, w_in, conv_w, conv_b, dt_bias, a_log, d_skip, norm_ssm, w_o_ssm, w_o_attn, w_out, norm_ffn2, w_ffn2_gate, w_ffn2_up, w_ffn2_down, norm_ple, w_ple_gate, w_ple_proj, norm_final):
    depth = w_in.shape[0]
    assert depth == 1
    layer_params = dict(
        norm_ffn1=norm_ffn1, w_ffn1_gate=w_ffn1_gate, w_ffn1_up=w_ffn1_up, w_ffn1_down=w_ffn1_down,
        norm_mix=norm_mix, w_in=w_in, conv_w=conv_w, conv_b=conv_b, dt_bias=dt_bias, a_log=a_log,
        d_skip=d_skip, norm_ssm=norm_ssm, w_o_ssm=w_o_ssm, w_o_attn=w_o_attn, w_out=w_out,
        norm_ffn2=norm_ffn2, w_ffn2_gate=w_ffn2_gate, w_ffn2_up=w_ffn2_up, w_ffn2_down=w_ffn2_down,
        norm_ple=norm_ple, w_ple_gate=w_ple_gate, w_ple_proj=w_ple_proj)
    i = 0
    w = _layer_weights({name: val[i] for name, val in layer_params.items()})
    yp, kvp, convp, ssmp = _prompt_layer(x_prompt, p_prompt[i], w, norm_final, PROMPT_ROW_TILE)
    ys, kvs, convs, ssms = _sample_layer(x_sample, p_sample[i], w, norm_final, state_conv[i], state_ssm[i],
                                         (cache_kv_w128[i], cache_kv_w512[i], cache_kv_w2048[i]))
    st = lambda a: a[None]
    return (yp, ys, st(kvp[0]), st(kvp[1]), st(kvp[2]), st(convp), st(ssmp),
            st(kvs[0]), st(kvs[1]), st(kvs[2]), st(convs), st(ssms))
```

```python
import functools
import math

import jax
import jax.numpy as jnp
from jax import lax
from jax.experimental import pallas as pl
from jax.experimental.pallas import tpu as pltpu

F32 = jnp.float32
BF16 = jnp.bfloat16

EPS = 1e-6
LOG2_E = math.log2(math.e)
ROPE_THETA = 10000.0
PAST_LEN = 8192
HEAD_DIM = 64
HPG = 8
SSM_GROUPS = 4
SSM_STATE = 128
CONV_W = 4
DIL_GROUPS = ((128, 1), (512, 4), (2048, 16))
ATTN_SCALE = HEAD_DIM ** -0.5
Q_LOG2_SCALE = ATTN_SCALE * LOG2_E
NK = 128

LANES = 128
SUBLANES = 8
VMEM_LIMIT_CAP = 56 * 1024 * 1024


def _cparams(sem, vmem_bytes):
    return pltpu.CompilerParams(dimension_semantics=sem,
                                vmem_limit_bytes=int(min(VMEM_LIMIT_CAP, vmem_bytes)))


def _resident(shape):
    nd = len(shape)
    return pl.BlockSpec(shape, lambda *_: (0,) * nd, pipeline_mode=pl.Buffered(1))


def _nbytes(shape, dtype):
    return math.prod(shape) * jnp.dtype(dtype).itemsize


def _rms(x, gain):
    ms = jnp.mean(x * x, axis=-1, keepdims=True)
    return x * lax.rsqrt(ms + EPS) * gain


def _sigmoid(x):
    return 1.0 / (1.0 + jnp.exp(-x))


def _silu(x):
    half = 0.5 * x
    return half + half * jnp.tanh(half)


def _softplus(x):
    return jnp.maximum(x, 0.0) + jnp.log1p(jnp.exp(-jnp.abs(x)))


def _dot(a, b):
    return jnp.dot(a, b, preferred_element_type=F32)


def _dot_nt(a, b):
    return lax.dot_general(a, b, (((1,), (1,)), ((), ())), preferred_element_type=F32)


def _split3(x):
    hi = x.astype(BF16)
    r1 = x - hi.astype(F32)
    mid = r1.astype(BF16)
    lo = (r1 - mid.astype(F32)).astype(BF16)
    return hi, mid, lo


def _dot_sel_l(sel, x):
    hi, mid, lo = _split3(x)
    return _dot(sel, hi) + _dot(sel, mid) + _dot(sel, lo)


def _dot_sel_r(x, sel):
    hi, mid, lo = _split3(x)
    return _dot(hi, sel) + _dot(mid, sel) + _dot(lo, sel)


def _head_expand(n_heads_padded, n_heads):
    r = lax.broadcasted_iota(jnp.int32, (n_heads_padded, n_heads * HEAD_DIM), 0)
    c = lax.broadcasted_iota(jnp.int32, (n_heads_padded, n_heads * HEAD_DIM), 1)
    return (c // HEAD_DIM == r).astype(BF16)


FF_CHUNK = 256


def _swiglu_half(x, gain, wg_ref, wu_ref, wd_ref):
    u = _rms(x, gain).astype(BF16)
    d_ff = wg_ref.shape[1]
    acc = jnp.zeros(x.shape, F32)
    for c in range(d_ff // FF_CHUNK):
        sl = slice(c * FF_CHUNK, (c + 1) * FF_CHUNK)
        g = _dot(u, wg_ref[:, sl])
        up = _dot(u, wu_ref[:, sl])
        acc = acc + _dot((_silu(g) * up).astype(BF16), wd_ref[sl, :])
    return x + 0.5 * acc


def _ffn_kernel(x_ref, n1_ref, wg_ref, wu_ref, wd_ref, n2_ref, h_ref, u_ref):
    h = _swiglu_half(x_ref[...], n1_ref[...], wg_ref, wu_ref, wd_ref)
    h_ref[...] = h
    u_ref[...] = _rms(h, n2_ref[...]).astype(u_ref.dtype)


def _ffn(x, n1, wg, wu, wd, n2, tm):
    t, d = x.shape
    d_ff = wg.shape[1]
    vmem = (3 * _nbytes((d, d_ff), BF16) + 2 * 2 * _nbytes((tm, d), F32) + 2 * _nbytes((tm, d), BF16)
            + 6 * _nbytes((tm, d), F32) + 4 * _nbytes((tm, FF_CHUNK), F32))
    return pl.pallas_call(
        _ffn_kernel,
        grid=(t // tm,),
        in_specs=[pl.BlockSpec((tm, d), lambda i: (i, 0)),
                  _resident((1, d)), _resident((d, d_ff)), _resident((d, d_ff)), _resident((d_ff, d)),
                  _resident((1, d))],
        out_specs=[pl.BlockSpec((tm, d), lambda i: (i, 0)), pl.BlockSpec((tm, d), lambda i: (i, 0))],
        out_shape=[jax.ShapeDtypeStruct((t, d), F32), jax.ShapeDtypeStruct((t, d), BF16)],
        compiler_params=_cparams(("parallel",), vmem),
        name="ffn_half_step",
    )(x, n1, wg, wu, wd, n2)


def _rope_tables(tm, seq, pos0):
    row = lax.broadcasted_iota(jnp.int32, (tm, LANES), 0) + pl.program_id(0) * tm
    pos = (row % seq + pos0).astype(F32)
    lane = lax.broadcasted_iota(jnp.int32, (tm, LANES), 1)
    half = HEAD_DIM // 2
    j = (lane % half).astype(F32)
    inv_freq = jnp.exp(j * (-math.log(ROPE_THETA) / half))
    ang = pos * inv_freq
    first = (lane % HEAD_DIM) < half
    return jnp.cos(ang), jnp.where(first, -jnp.sin(ang), jnp.sin(ang)), first


def _rotary(x, cos, sin_signed, first):
    outs = []
    for c in range(x.shape[1] // LANES):
        xb = x[:, c * LANES:(c + 1) * LANES]
        other = jnp.where(first, pltpu.roll(xb, LANES - HEAD_DIM // 2, 1), pltpu.roll(xb, HEAD_DIM // 2, 1))
        outs.append(xb * cos + other * sin_signed)
    return jnp.concatenate(outs, axis=1)


def _proj_ssm_kernel(u_ref, wz_ref, wx_ref, wdt_ref, z_ref, xbc_ref, dt_ref):
    u = u_ref[...]
    z_ref[...] = _dot(u, wz_ref[...]).astype(z_ref.dtype)
    xbc_ref[...] = _dot(u, wx_ref[...]).astype(xbc_ref.dtype)
    dt_ref[...] = _dot(u, wdt_ref[...])


def _proj_attn_kernel(u_ref, wq_ref, wk_ref, wv_ref, wgs_ref, wga_ref,
                      q_ref, k_ref, v_ref, gs_ref, ga_ref, *, seq, pos0):
    u = u_ref[...]
    cos, sin_signed, first = _rope_tables(u.shape[0], seq, pos0)
    q_ref[...] = _rotary(_dot(u, wq_ref[...]), cos, sin_signed, first).astype(q_ref.dtype)
    k_ref[...] = _rotary(_dot(u, wk_ref[...]), cos, sin_signed, first).astype(k_ref.dtype)
    v_ref[...] = _dot(u, wv_ref[...]).astype(v_ref.dtype)
    gs_ref[...] = _sigmoid(_dot(u, wgs_ref[...])).astype(gs_ref.dtype)
    ga_ref[...] = _sigmoid(_dot(u, wga_ref[...])).astype(ga_ref.dtype)


GROUP_WIDTH = HPG * HEAD_DIM
_STAGED_GROUPS = [gi for gi, (_, dil) in enumerate(DIL_GROUPS) if dil > 1]


ROW_SLAB = 128


def _rope_store_wide(res, rope, scale, cos_ref, sin_ref, first, stage, out_ref, dil):
    tm = res.shape[0]
    for rb in range(tm // ROW_SLAB):
        rows = slice(rb * ROW_SLAB, (rb + 1) * ROW_SLAB)
        for j in range(GROUP_WIDTH // LANES):
            lanes = slice(j * LANES, (j + 1) * LANES)
            xb = res[rows, lanes]
            if rope:
                other = jnp.where(first, pltpu.roll(xb, LANES - HEAD_DIM // 2, 1), pltpu.roll(xb, HEAD_DIM // 2, 1))
                xb = xb * cos_ref[rows, :] + other * sin_ref[rows, :]
            if scale is not None:
                xb = xb * scale
            if dil == 1:
                out_ref[rows, lanes] = xb.astype(out_ref.dtype)
            else:
                stage[j, rows, :] = xb
    if dil > 1:
        n_rows = tm // dil
        for r in range(dil):
            for j in range(GROUP_WIDTH // LANES):
                lo = r * GROUP_WIDTH + j * LANES
                out_ref[:, lo:lo + LANES] = stage[j, pl.ds(r, n_rows, stride=dil), :].astype(out_ref.dtype)


def _rope_table_kernel(cos_ref, sin_ref, *, seq):
    cos, sin_signed, _ = _rope_tables(cos_ref.shape[0], seq, 0)
    cos_ref[...] = cos
    sin_ref[...] = sin_signed


def _rope_table(seq, tm):
    spec = pl.BlockSpec((tm, LANES), lambda i: (i, 0))
    shape = jax.ShapeDtypeStruct((seq, LANES), F32)
    return pl.pallas_call(
        functools.partial(_rope_table_kernel, seq=seq),
        grid=(seq // tm,), in_specs=[], out_specs=[spec, spec], out_shape=[shape, shape],
        compiler_params=_cparams(("parallel",), 16 * _nbytes((tm, LANES), F32)),
        name="rope_table",
    )()


def _proj_attn_wide_kernel(u_ref, cos_ref, sin_ref, wq_ref, wk_ref, wv_ref, wgs_ref, wga_ref, *refs):
    n_grp = len(DIL_GROUPS)
    q_refs, k_refs, v_refs = refs[0:n_grp], refs[n_grp:2 * n_grp], refs[2 * n_grp:3 * n_grp]
    gs_ref, ga_ref, stage = refs[3 * n_grp:]
    u = u_ref[...]
    first = lax.broadcasted_iota(jnp.int32, (ROW_SLAB, LANES), 1) % HEAD_DIM < HEAD_DIM // 2
    for ti, (w_ref, out_refs, rope, scale) in enumerate(((wq_ref, q_refs, True, Q_LOG2_SCALE),
                                                        (wk_ref, k_refs, True, None),
                                                        (wv_ref, v_refs, False, None))):
        for gi, (_, dil) in enumerate(DIL_GROUPS):
            res = _dot(u, w_ref[:, gi * GROUP_WIDTH:(gi + 1) * GROUP_WIDTH])
            slot = None if dil == 1 else stage.at[ti * len(_STAGED_GROUPS) + _STAGED_GROUPS.index(gi)]
            _rope_store_wide(res, rope, scale, cos_ref, sin_ref, first, slot, out_refs[gi], dil)
    gs_ref[...] = _sigmoid(_dot(u, wgs_ref[...])).astype(gs_ref.dtype)
    ga_ref[...] = _sigmoid(_dot(u, wga_ref[...])).astype(ga_ref.dtype)


def _proj_attn_wide(u, weights, seq, tm):
    t, d = u.shape
    d_model = weights[3].shape[1]
    row = lambda i: (i, 0)
    qkv_specs = [pl.BlockSpec((tm // dil, dil * GROUP_WIDTH), row) for _, dil in DIL_GROUPS] * 3
    qkv_shapes = [jax.ShapeDtypeStruct((t // dil, dil * GROUP_WIDTH), BF16) for _, dil in DIL_GROUPS] * 3
    n_stage = 3 * len(_STAGED_GROUPS)
    vmem = (sum(_nbytes(w.shape, BF16) for w in weights) + 2 * _nbytes((tm, d), BF16)
            + 2 * _nbytes((tm, 3 * len(DIL_GROUPS) * GROUP_WIDTH + 2 * d_model), BF16)
            + 4 * _nbytes((tm, LANES), F32)
            + n_stage * _nbytes((tm, GROUP_WIDTH), F32) + 6 * _nbytes((tm, weights[0].shape[1]), F32))
    cos, sin_signed = _rope_table(seq, tm)
    tiles_per_seq = seq // tm
    table = pl.BlockSpec((tm, LANES), lambda i: (i % tiles_per_seq, 0))
    outs = pl.pallas_call(
        _proj_attn_wide_kernel,
        grid=(t // tm,),
        in_specs=[pl.BlockSpec((tm, d), row), table, table] + [_resident(w.shape) for w in weights],
        out_specs=qkv_specs + [pl.BlockSpec((tm, d_model), row)] * 2,
        out_shape=qkv_shapes + [jax.ShapeDtypeStruct((t, d_model), BF16)] * 2,
        scratch_shapes=[pltpu.VMEM((n_stage, GROUP_WIDTH // LANES, tm, LANES), F32)],
        compiler_params=_cparams(("parallel",), vmem),
        name="proj_attn",
    )(u, cos, sin_signed, *weights)
    n_grp = len(DIL_GROUPS)
    return outs[0:n_grp], outs[n_grp:2 * n_grp], outs[2 * n_grp:3 * n_grp], outs[3 * n_grp], outs[3 * n_grp + 1]


def _proj(kernel, u, weights, out_dtypes, tm, name):
    t, d = u.shape
    row = lambda i: (i, 0)
    widths = [w.shape[1] for w in weights]
    vmem = (sum(_nbytes(w.shape, BF16) for w in weights) + 2 * _nbytes((tm, d), BF16)
            + sum(2 * _nbytes((tm, n), dt) for n, dt in zip(widths, out_dtypes))
            + 4 * _nbytes((tm, max(widths)), F32))
    return pl.pallas_call(
        kernel,
        grid=(t // tm,),
        in_specs=[pl.BlockSpec((tm, d), row)] + [_resident(w.shape) for w in weights],
        out_specs=[pl.BlockSpec((tm, n), row) for n in widths],
        out_shape=[jax.ShapeDtypeStruct((t, n), dt) for n, dt in zip(widths, out_dtypes)],
        compiler_params=_cparams(("parallel",), vmem),
        name=name,
    )(u, *weights)


SSD_CHUNK = 128
SSD_CHUNKS_PER_STEP = 4


def _ssd_gate_norm(y, z, gain):
    return _rms(y * _silu(z), gain)


def _ssd_chunk(x_prev, x_cur, z, dt_raw, states, cw_ref, cb_ref, dtb_ref, alog_ref, dskip_ref, ng_ref,
               shift_ref, expand_ref):
    L = SSD_CHUNK
    d_inner = z.shape[-1]
    gw = HPG * HEAD_DIM
    gn = SSM_GROUPS * SSM_STATE

    both = jnp.concatenate([x_prev, x_cur], axis=0)
    conv = cb_ref[...] + x_cur.astype(F32) * cw_ref[CONV_W - 1:CONV_W, :]
    shifted = _dot(shift_ref[...], both)
    for back in range(1, CONV_W):
        tap = CONV_W - 1 - back
        conv = conv + shifted[(back - 1) * L:back * L, :] * cw_ref[tap:tap + 1, :]
    xa = _silu(conv)
    xs = xa[:, :d_inner]
    bm = xa[:, d_inner:d_inner + gn]
    cm = xa[:, d_inner + gn:]

    dt = _softplus(dt_raw + dtb_ref[...])
    a = -jnp.exp(alog_ref[...])
    row = lax.broadcasted_iota(jnp.int32, (L, L), 0)
    col = lax.broadcasted_iota(jnp.int32, (L, L), 1)
    causal = row >= col
    acs = _dot_sel_l(causal.astype(BF16), dt * a)
    acs_log2 = acs * LOG2_E
    acs_log2_t = acs_log2.T
    acs_last = acs[L - 1:L, :]

    expand = expand_ref[...]
    per_head = jnp.concatenate([jnp.exp(acs), jnp.exp(acs_last - acs) * dt, dt], axis=0)
    per_lane = _dot(per_head.astype(BF16), expand)
    e_acs, w_end, dt_lane = per_lane[:L], per_lane[L:2 * L], per_lane[2 * L:]
    dec_state = _dot_sel_r(jnp.broadcast_to(jnp.exp(acs_last), (SUBLANES, LANES)), expand)[0:1, :]

    xs_b = (xs * dt_lane).astype(BF16)
    xw = (xs * w_end).astype(BF16)
    lane_first = lax.broadcasted_iota(jnp.int32, (L, LANES), 1) < HEAD_DIM
    y_groups, new_states = [], []
    for g in range(SSM_GROUPS):
        gsl = slice(g * gw, (g + 1) * gw)
        b_g = bm[:, g * SSM_STATE:(g + 1) * SSM_STATE]
        c_g = cm[:, g * SSM_STATE:(g + 1) * SSM_STATE].astype(BF16)
        cb = _dot_nt(c_g, b_g.astype(BF16))
        st_g = states[g]
        y_inter = _dot(c_g, st_g.astype(BF16))
        pairs = []
        for pr in range(HPG // 2):
            w_pair = []
            for sub in range(2):
                h = g * HPG + pr * 2 + sub
                seg = acs_log2[:, h:h + 1] - acs_log2_t[h:h + 1, :]
                w = cb * jnp.exp2(jnp.where(causal, seg, -jnp.inf))
                w_pair.append(w.astype(BF16))
            lo = g * gw + pr * LANES
            res = _dot(jnp.concatenate(w_pair, axis=0), xs_b[:, lo:lo + LANES])
            pairs.append(jnp.where(lane_first, res[:L], res[L:]))
        y_groups.append(jnp.concatenate(pairs, axis=1) + y_inter * e_acs[:, gsl])
        new_states.append(dec_state[:, gsl] * st_g + _dot(b_g.T.astype(BF16), xw[:, gsl]))
    y = jnp.concatenate(y_groups, axis=1) + dskip_ref[...] * xs
    return _ssd_gate_norm(y, z.astype(F32), ng_ref[...]), new_states


def _ssd_prompt_kernel(z_ref, xprev_ref, xbc_ref, dt_ref, cw_ref, cb_ref, dtb_ref, alog_ref, dskip_ref, ng_ref,
                       shift_ref, expand_ref, y_ref, st_ref, st_t):
    L = SSD_CHUNK
    c = pl.program_id(1)

    @pl.when(c == 0)
    def _():
        st_t[...] = jnp.zeros(st_t.shape, F32)

    states = [st_t[g] for g in range(SSM_GROUPS)]
    x_prev = jnp.where(c > 0, xprev_ref[0], jnp.zeros(xprev_ref.shape[1:], xprev_ref.dtype))
    for j in range(xbc_ref.shape[1] // L):
        rows = slice(j * L, (j + 1) * L)
        x_cur = xbc_ref[0, rows, :]
        y, states = _ssd_chunk(x_prev, x_cur, z_ref[0, rows, :], dt_ref[0, rows, :], states,
                               cw_ref, cb_ref, dtb_ref, alog_ref, dskip_ref, ng_ref, shift_ref, expand_ref)
        y_ref[0, rows, :] = y.astype(y_ref.dtype)
        x_prev = x_cur
    for g in range(SSM_GROUPS):
        st_t[g] = states[g]

    @pl.when(c == pl.num_programs(1) - 1)
    def _():
        for g in range(SSM_GROUPS):
            st_ref[0, g * HPG:(g + 1) * HPG] = states[g].T.reshape(HPG, HEAD_DIM, SSM_STATE)


def _ssd_prompt(z, xbc, dt, cw, cb, dtb, alog, dskip, ng):
    b, s, d_inner = z.shape
    conv_dim = xbc.shape[-1]
    n_heads = d_inner // HEAD_DIM
    L = SSD_CHUNK
    per_step = math.gcd(s // L, SSD_CHUNKS_PER_STEP)
    blk = lambda n: pl.BlockSpec((1, per_step * L, n), lambda i, c: (i, c, 0))
    r = jnp.arange(L)[:, None]
    cidx = jnp.arange(2 * L)[None, :]
    shift = jnp.concatenate([(cidx == r + L - k) for k in range(1, CONV_W)], axis=0).astype(BF16)
    heads = jnp.arange(LANES)[:, None]
    expand = (jnp.arange(d_inner)[None, :] // HEAD_DIM == heads).astype(BF16)
    prev_blk = pl.BlockSpec((1, L, conv_dim), lambda i, c: (i, jnp.maximum(c * per_step - 1, 0), 0))
    vmem = (2 * per_step * (_nbytes((L, d_inner), z.dtype) * 2 + _nbytes((L, conv_dim), xbc.dtype)
                            + _nbytes((L, LANES), F32))
            + 2 * _nbytes((L, conv_dim), xbc.dtype)
            + 2 * _nbytes((n_heads, HEAD_DIM, SSM_STATE), F32)
            + _nbytes((SSM_GROUPS, SSM_STATE, HPG * HEAD_DIM), F32)
            + _nbytes(shift.shape, BF16) + _nbytes(expand.shape, BF16)
            + 16 * per_step * _nbytes((L, conv_dim), F32))
    return pl.pallas_call(
        _ssd_prompt_kernel,
        grid=(b, s // (per_step * L)),
        in_specs=[blk(d_inner), prev_blk, blk(conv_dim), blk(LANES),
                  _resident(cw.shape), _resident(cb.shape), _resident(dtb.shape), _resident(alog.shape),
                  _resident(dskip.shape), _resident(ng.shape), _resident(shift.shape), _resident(expand.shape)],
        out_specs=[blk(d_inner),
                   pl.BlockSpec((1, n_heads, HEAD_DIM, SSM_STATE), lambda i, c: (i, 0, 0, 0))],
        out_shape=[jax.ShapeDtypeStruct((b, s, d_inner), BF16),
                   jax.ShapeDtypeStruct((b, n_heads, HEAD_DIM, SSM_STATE), F32)],
        scratch_shapes=[pltpu.VMEM((SSM_GROUPS, SSM_STATE, HPG * HEAD_DIM), F32)],
        compiler_params=_cparams(("parallel", "arbitrary"), vmem),
        name="ssd_prompt_scan",
    )(z, xbc, xbc, dt, cw, cb, dtb, alog, dskip, ng, shift, expand)


MERGE_FAN = 4
ATTN_BLOCKS_PER_STEP = 8


def _attn_prompt_kernel(*refs, first, last):
    q_ref, kp_ref, kc_ref, vp_ref, vc_ref = refs[:5]
    refs = refs[5:]
    if not first:
        oin_refs, lin_refs = refs[:MERGE_FAN], refs[MERGE_FAN:2 * MERGE_FAN]
        refs = refs[2 * MERGE_FAN:]
    out_refs = refs
    o_ref = out_refs[0]
    n = pl.program_id(2)
    n_blocks = q_ref.shape[0] // NK
    qi = lax.broadcasted_iota(jnp.int32, (NK, 2 * NK), 0)
    kj = lax.broadcasted_iota(jnp.int32, (NK, 2 * NK), 1)
    dist = qi + NK - kj
    band = (dist >= 0) & (dist <= NK)
    lane = lax.broadcasted_iota(jnp.int32, (NK, LANES), 1)
    lane_first = lane < HEAD_DIM
    if not first:
        rows = NK // MERGE_FAN
        dst = lax.broadcasted_iota(jnp.int32, (NK, NK), 0)
        src = lax.broadcasted_iota(jnp.int32, (NK, NK), 1)
        perm = (src == (dst % MERGE_FAN) * rows + dst // MERGE_FAN).astype(BF16)
        expand = _head_expand(LANES, HPG)
    for j in range(n_blocks):
        own = slice(j * NK, (j + 1) * NK)
        q = q_ref[own, :]
        if j == 0:
            k_before, v_before = kp_ref[...], vp_ref[...]
            mask = band & ((kj >= NK) | (n > 0))
        else:
            before = slice((j - 1) * NK, j * NK)
            k_before, v_before = kc_ref[before, :], vc_ref[before, :]
            mask = band
        kk = jnp.concatenate([k_before, kc_ref[own, :]], axis=0)
        vv = jnp.concatenate([v_before, vc_ref[own, :]], axis=0)
        lse_out = jnp.zeros((NK, LANES), F32)
        o_parts = []
        for pr in range(HPG // 2):
            sl = slice(pr * LANES, (pr + 1) * LANES)
            q_pair, k_pair, v_pair = q[:, sl], kk[:, sl], vv[:, sl]
            res = []
            for sub in range(2):
                h = pr * 2 + sub
                qm = jnp.where(lane_first if sub == 0 else ~lane_first, q_pair, jnp.zeros_like(q_pair))
                s = jnp.where(mask, _dot_nt(qm, k_pair), -jnp.inf)
                m = jnp.max(s, axis=-1, keepdims=True)
                p = jnp.exp2(s - m)
                l = jnp.sum(p, axis=-1, keepdims=True)
                res.append(_dot(p.astype(BF16), v_pair) / l)
                lse_out = jnp.where(lane == h, m + jnp.log2(l), lse_out)
            o_parts.append(jnp.where(lane_first, res[0], res[1]))
        o = jnp.concatenate(o_parts, axis=1)
        if not first:
            part = slice(j * rows, (j + 1) * rows)
            o_prev = _dot(perm, jnp.concatenate([r[part, :] for r in oin_refs], axis=0))
            lse_prev = _dot_sel_l(perm, jnp.concatenate([r[part, :] for r in lin_refs], axis=0))
            lse_new = jnp.maximum(lse_out, lse_prev) + jnp.log2(1.0 + jnp.exp2(-jnp.abs(lse_out - lse_prev)))
            w_prev = _dot_sel_r(jnp.exp2(lse_prev - lse_new), expand)
            w_cur = _dot_sel_r(jnp.exp2(lse_out - lse_new), expand)
            o = o_prev * w_prev + o * w_cur
            lse_out = lse_new
        o_ref[own, :] = o.astype(o_ref.dtype)
        if not last:
            out_refs[1][own, :] = lse_out


def _attn_prompt_group(q, k, v, batch, gi, first, last, o_in, lse_in):
    window, dil = DIL_GROUPS[gi]
    td = q.shape[0]
    sd = td // batch
    assert window // dil == NK and sd % NK == 0 and q.shape[1] == dil * GROUP_WIDTH
    nb = sd // NK
    g = math.gcd(nb, ATTN_BLOCKS_PER_STEP)
    ns = nb // g
    blk = lambda width: pl.BlockSpec((g * NK, width), lambda i, r, n: (i * ns + n, r))
    prev = pl.BlockSpec((NK, GROUP_WIDTH), lambda i, r, n: (i * nb + jnp.maximum(n * g - 1, 0), r))
    args, in_specs = [q, k, k, v, v], [blk(GROUP_WIDTH), prev, blk(GROUP_WIDTH), prev, blk(GROUP_WIDTH)]
    if not first:
        assert o_in.shape == (td // MERGE_FAN, MERGE_FAN * dil * GROUP_WIDTH)
        piece = lambda width, c: pl.BlockSpec((g * NK // MERGE_FAN, width),
                                              lambda i, r, n: (i * ns + n, dil * c + r))
        args += [o_in] * MERGE_FAN + [lse_in] * MERGE_FAN
        in_specs += [piece(GROUP_WIDTH, c) for c in range(MERGE_FAN)] + [piece(LANES, c) for c in range(MERGE_FAN)]
    out_specs = [blk(GROUP_WIDTH)]
    out_shape = [jax.ShapeDtypeStruct((td, dil * GROUP_WIDTH), BF16)]
    if not last:
        out_specs.append(blk(LANES))
        out_shape.append(jax.ShapeDtypeStruct((td, dil * LANES), F32))
    vmem = (2 * (6 * g + 2) * _nbytes((NK, GROUP_WIDTH), BF16) + 6 * g * _nbytes((NK, LANES), F32)
            + 24 * _nbytes((NK, 2 * NK), F32) + 6 * g * _nbytes((NK, GROUP_WIDTH), F32))
    outs = pl.pallas_call(
        functools.partial(_attn_prompt_kernel, first=first, last=last),
        grid=(batch, dil, ns),
        in_specs=in_specs, out_specs=out_specs, out_shape=out_shape,
        compiler_params=_cparams(("parallel", "parallel", "parallel"), vmem),
        name=f"attn_prompt_w{window}",
    )(*args)
    return outs[0], (None if last else outs[1])


def _out_kernel(h_ref, y_ref, at_ref, gs_ref, ga_ref, p_ref,
                wos_ref, woa_ref, wo_ref, n2_ref, wg_ref, wu_ref, wd_ref, np_ref, wpg_ref, wpp_ref, nf_ref,
                o_ref):
    merged = (gs_ref[...].astype(F32) * _dot(y_ref[...].astype(BF16), wos_ref[...])
              + ga_ref[...].astype(F32) * _dot(at_ref[...].astype(BF16), woa_ref[...]))
    h = h_ref[...] + _dot(merged.astype(BF16), wo_ref[...])
    h = _swiglu_half(h, n2_ref[...], wg_ref, wu_ref, wd_ref)
    gate = _sigmoid(_dot(_rms(h, np_ref[...]).astype(BF16), wpg_ref[...]))
    h = h + gate * _dot(p_ref[...].astype(BF16), wpp_ref[...])
    o_ref[...] = _rms(h, nf_ref[...])


def _out_stage(h, y, attn, gs, ga, p, weights, tm):
    t, d = h.shape
    row = lambda i: (i, 0)
    acts = [h, y, attn, gs, ga, p]
    vmem = (sum(_nbytes(w.shape, w.dtype) for w in weights)
            + sum(2 * _nbytes((tm, a.shape[1]), a.dtype) for a in acts) + 2 * _nbytes((tm, d), F32)
            + 8 * _nbytes((tm, d), F32) + 4 * _nbytes((tm, FF_CHUNK), F32))
    return pl.pallas_call(
        _out_kernel,
        grid=(t // tm,),
        in_specs=[pl.BlockSpec((tm, a.shape[1]), row) for a in acts] + [_resident(w.shape) for w in weights],
        out_specs=pl.BlockSpec((tm, d), row),
        out_shape=jax.ShapeDtypeStruct((t, d), F32),
        compiler_params=_cparams(("parallel",), vmem),
        name="output_stage",
    )(*acts, *weights)


def _ssd_sample_prep_kernel(xn_ref, sc_ref, dtr_ref, cw_ref, cb_ref, dtb_ref, alog_ref,
                            xs_ref, xdt_t_ref, bdt_ref, c_t_ref, dec_ref):
    conv_dim = xn_ref.shape[1]
    d_inner = xs_ref.shape[1]
    gn = SSM_GROUPS * SSM_STATE
    conv = cb_ref[...] + xn_ref[...] * cw_ref[CONV_W - 1:CONV_W, :]
    for tap in range(CONV_W - 1):
        conv = conv + sc_ref[:, tap * conv_dim:(tap + 1) * conv_dim] * cw_ref[tap:tap + 1, :]
    xa = _silu(conv)
    xs = xa[:, :d_inner]
    xs_ref[...] = xs
    dt = _softplus(dtr_ref[...] + dtb_ref[...])
    dec_ref[...] = jnp.exp(dt * -jnp.exp(alog_ref[...]))
    n_heads = d_inner // HEAD_DIM
    dt_x = _dot_sel_r(dt, _head_expand(LANES, n_heads))
    xdt_t_ref[...] = (xs * dt_x).T.astype(xdt_t_ref.dtype)
    bdt_ref[...] = xa[:, d_inner:d_inner + gn]
    for g in range(SSM_GROUPS):
        lo = d_inner + gn + g * SSM_STATE
        c_t_ref[g] = xa[:, lo:lo + SSM_STATE].T.astype(c_t_ref.dtype)


def _ssd_sample_prep(xn, sc, dtr, cw, cb, dtb, alog, d_inner):
    nb, conv_dim = xn.shape
    args = [xn, sc, dtr, cw, cb, dtb, alog]
    out_shape = [jax.ShapeDtypeStruct((nb, d_inner), F32),
                 jax.ShapeDtypeStruct((d_inner, nb), BF16),
                 jax.ShapeDtypeStruct((nb, SSM_GROUPS * SSM_STATE), F32),
                 jax.ShapeDtypeStruct((SSM_GROUPS, SSM_STATE, nb), BF16),
                 jax.ShapeDtypeStruct((nb, LANES), F32)]
    vmem = 2 * sum(_nbytes(a.shape, a.dtype) for a in args) + 2 * sum(_nbytes(o.shape, o.dtype) for o in out_shape) \
        + 8 * _nbytes((nb, conv_dim), F32)
    return pl.pallas_call(
        _ssd_sample_prep_kernel,
        grid=(1,),
        in_specs=[_resident(a.shape) for a in args],
        out_specs=[pl.BlockSpec(o.shape, functools.partial(lambda nd, i: (0,) * nd, len(o.shape))) for o in out_shape],
        out_shape=out_shape,
        compiler_params=_cparams(("arbitrary",), vmem),
        name="ssd_sample_prep",
    )(*args)


SAMPLE_STATE_TILE = 4


def _ssd_sample_state_kernel(dec_ref, st_ref, xdt_t_ref, bdt_ref, c_t_ref, sto_ref, y_t_ref):
    i = pl.program_id(0)
    nb = bdt_ref.shape[0]
    n_heads = st_ref.shape[1]
    gw = HPG * HEAD_DIM

    @pl.when(i == 0)
    def _():
        y_t_ref[...] = jnp.zeros(y_t_ref.shape, F32)

    row = lax.broadcasted_iota(jnp.int32, (nb, SSM_STATE), 0)
    lane = lax.broadcasted_iota(jnp.int32, (gw, nb), 1)
    for j in range(SAMPLE_STATE_TILE):
        b = i * SAMPLE_STATE_TILE + j
        for g in range(SSM_GROUPS):
            b_row = jnp.where(row == b, bdt_ref[:, g * SSM_STATE:(g + 1) * SSM_STATE], 0.0).astype(BF16)
            upd = _dot(xdt_t_ref[g * gw:(g + 1) * gw, :], b_row)
            new = []
            for hg in range(HPG):
                h = g * HPG + hg
                new_h = st_ref[j, h] * dec_ref[b * n_heads + h] + upd[hg * HEAD_DIM:(hg + 1) * HEAD_DIM, :]
                sto_ref[j, h] = new_h
                new.append(new_h)
            yg = _dot(jnp.concatenate(new, axis=0).astype(BF16), c_t_ref[g])
            y_t_ref[g * gw:(g + 1) * gw, :] += jnp.where(lane == b, yg, 0.0)


def _ssd_sample_state(dec, st, xdt_t, bdt, c_t):
    nb, n_heads = st.shape[0], st.shape[1]
    d_inner = n_heads * HEAD_DIM
    bt = SAMPLE_STATE_TILE
    st_spec = pl.BlockSpec((bt, n_heads, HEAD_DIM, SSM_STATE), lambda i: (i, 0, 0, 0))
    vmem = (4 * _nbytes((bt, n_heads, HEAD_DIM, SSM_STATE), F32) + _nbytes(xdt_t.shape, BF16)
            + _nbytes(bdt.shape, F32) + _nbytes(c_t.shape, BF16) + 2 * _nbytes((d_inner, nb), F32)
            + 16 * _nbytes((HPG * HEAD_DIM, SSM_STATE), F32))
    return pl.pallas_call(
        _ssd_sample_state_kernel,
        grid=(nb // bt,),
        in_specs=[pl.BlockSpec(memory_space=pltpu.SMEM), st_spec,
                  _resident(xdt_t.shape), _resident(bdt.shape), _resident(c_t.shape)],
        out_specs=[st_spec, pl.BlockSpec((d_inner, nb), lambda i: (0, 0))],
        out_shape=[jax.ShapeDtypeStruct(st.shape, F32), jax.ShapeDtypeStruct((d_inner, nb), F32)],
        compiler_params=_cparams(("arbitrary",), vmem),
        name="ssd_sample_state",
    )(dec, st, xdt_t, bdt, c_t)


def _ssd_sample_post_kernel(y_t_ref, xs_ref, z_ref, dskip_ref, ng_ref, y_ref):
    y = y_t_ref[...].T + dskip_ref[...] * xs_ref[...]
    y_ref[...] = _ssd_gate_norm(y, z_ref[...], ng_ref[...]).astype(y_ref.dtype)


def _ssd_sample_post(y_t, xs, z, dskip, ng):
    args = [y_t, xs, z, dskip, ng]
    vmem = 2 * sum(_nbytes(a.shape, a.dtype) for a in args) + 8 * _nbytes(xs.shape, F32)
    return pl.pallas_call(
        _ssd_sample_post_kernel,
        grid=(1,),
        in_specs=[_resident(a.shape) for a in args],
        out_specs=pl.BlockSpec(xs.shape, lambda i: (0, 0)),
        out_shape=jax.ShapeDtypeStruct(xs.shape, F32),
        compiler_params=_cparams(("arbitrary",), vmem),
        name="ssd_sample_post",
    )(*args)


def _attn_sample_kernel(q_ref, qt_ref, k_ref, v_ref, c0_ref, c1_ref, c2_ref, o_ref):
    q_rows = q_ref[0] * ATTN_SCALE
    q_cols = qt_ref[0] * ATTN_SCALE
    s_new_all = jnp.sum(k_ref[0] * q_rows, axis=-1, keepdims=True)
    scores = []
    for gi, (c_ref, (_, dil)) in enumerate(zip((c0_ref, c1_ref, c2_ref), DIL_GROUPS)):
        rows = []
        for h in range(HPG):
            col = gi * HPG + h
            rows.append(jnp.sum(c_ref[0, 0, h] * q_cols[:, col:col + 1], axis=0, keepdims=True))
        s = jnp.concatenate(rows, axis=0)
        pos = lax.broadcasted_iota(jnp.int32, s.shape, 1)
        scores.append((jnp.where(pos % dil == 0, s, -jnp.inf), s_new_all[gi * HPG:(gi + 1) * HPG]))
    m = functools.reduce(jnp.maximum, [jnp.maximum(jnp.max(s, axis=1, keepdims=True), sn) for s, sn in scores])
    den = jnp.zeros((HPG, 1), F32)
    o_new = jnp.zeros((HPG, HEAD_DIM), F32)
    o_cache_t = jnp.zeros((HEAD_DIM, LANES), F32)
    lane = lax.broadcasted_iota(jnp.int32, (HEAD_DIM, LANES), 1)
    for gi, ((s, sn), c_ref) in enumerate(zip(scores, (c0_ref, c1_ref, c2_ref))):
        p = jnp.exp(s - m)
        p_n = jnp.exp(sn - m)
        den = den + jnp.sum(p, axis=1, keepdims=True) + p_n
        o_new = o_new + p_n * v_ref[0, gi * HPG:(gi + 1) * HPG, :]
        for h in range(HPG):
            contrib = jnp.sum(c_ref[0, 1, h] * p[h:h + 1, :], axis=1, keepdims=True)
            o_cache_t = jnp.where(lane == h, o_cache_t + contrib, o_cache_t)
    o_cache = jnp.concatenate([o_cache_t, jnp.zeros_like(o_cache_t)], axis=0).T
    o_ref[0] = (o_cache[:HPG, :HEAD_DIM] + o_new) / den


def _attn_sample(q, k, v, caches):
    nb, n_heads, _ = q.shape
    views, specs = [], []
    for (window, dil), c in zip(DIL_GROUPS, caches):
        assert c.shape[1] == window and window // dil == NK
        views.append(jnp.transpose(c, (0, 2, 3, 4, 1)))
        specs.append(pl.BlockSpec((1, 2, HPG, HEAD_DIM, window), lambda i: (i, 0, 0, 0, 0)))
    vec = pl.BlockSpec((1, n_heads, HEAD_DIM), lambda i: (i, 0, 0))
    vec_t = pl.BlockSpec((1, HEAD_DIM, n_heads), lambda i: (i, 0, 0))
    cache_bytes = sum(_nbytes((2, HPG, HEAD_DIM, window), F32) for window, _ in DIL_GROUPS)
    vmem = 2 * cache_bytes + 6 * _nbytes((HEAD_DIM, DIL_GROUPS[-1][0]), F32) + 8 * _nbytes((LANES, LANES), F32)
    return pl.pallas_call(
        _attn_sample_kernel,
        grid=(nb,),
        in_specs=[vec, vec_t, vec, vec] + specs,
        out_specs=pl.BlockSpec((1, HPG, HEAD_DIM), lambda i: (i, 0, 0)),
        out_shape=jax.ShapeDtypeStruct((nb, HPG, HEAD_DIM), F32),
        compiler_params=_cparams(("parallel",), vmem),
        name="attn_sample",
    )(q, jnp.swapaxes(q, 1, 2), k, v, *views)


def _row(v, width=None):
    v = v.astype(F32).reshape(1, -1)
    if width is not None and v.shape[1] < width:
        v = jnp.pad(v, ((0, 0), (0, width - v.shape[1])))
    return v


def _layer_weights(prm):
    d_model = prm['w_in'].shape[0]
    d_inner = prm['norm_ssm'].shape[0]
    n_heads = d_inner // HEAD_DIM
    conv_dim = prm['conv_w'].shape[1]
    attn_w = len(DIL_GROUPS) * HPG * HEAD_DIM
    splits = [d_inner, conv_dim, n_heads, attn_w, attn_w, attn_w, d_model, d_model]
    offs = [0]
    for n in splits:
        offs.append(offs[-1] + n)
    w_in = prm['w_in']
    seg = [w_in[:, offs[j]:offs[j + 1]].astype(BF16) for j in range(len(splits))]
    seg[2] = jnp.pad(seg[2], ((0, 0), (0, LANES - n_heads)))
    bf = lambda name: prm[name].astype(BF16)
    return dict(
        ffn1=(_row(prm['norm_ffn1']), bf('w_ffn1_gate'), bf('w_ffn1_up'), bf('w_ffn1_down'), _row(prm['norm_mix'])),
        proj_ssm=seg[0:3], proj_attn=seg[3:8],
        conv_w=prm['conv_w'].astype(F32), conv_b=_row(prm['conv_b']),
        dt_bias=_row(prm['dt_bias'], LANES), a_log=_row(prm['a_log'], LANES),
        d_skip=_row(jnp.repeat(prm['d_skip'], HEAD_DIM)), norm_ssm=_row(prm['norm_ssm']),
        out=(bf('w_o_ssm'), bf('w_o_attn'), bf('w_out'), _row(prm['norm_ffn2']), bf('w_ffn2_gate'),
             bf('w_ffn2_up'), bf('w_ffn2_down'), _row(prm['norm_ple']), bf('w_ple_gate'), bf('w_ple_proj')),
        d_inner=d_inner, conv_dim=conv_dim,
    )


def _kv_stack(k, v, gi, keep):
    b, s, _ = k.shape
    gw = HPG * HEAD_DIM
    sel = lambda t: t[:, s - keep:, gi * gw:(gi + 1) * gw].astype(F32).reshape(b, keep, HPG, HEAD_DIM)
    return jnp.stack([sel(k), sel(v)], axis=2)


def _kv_stack_wide(k, v, batch, dil, keep):
    rows = k.shape[0] // batch
    sel = lambda t: (t.reshape(batch, rows, dil * GROUP_WIDTH)[:, rows - keep // dil:, :]
                     .astype(F32).reshape(batch, keep, HPG, HEAD_DIM))
    return jnp.stack([sel(k), sel(v)], axis=2)


def _prompt_layer(x, p_emb, w, norm_final, tm):
    b, s, d = x.shape
    t = b * s
    h1, u = _ffn(x.reshape(t, d), *w['ffn1'], tm)
    z, xbc, dt = _proj(_proj_ssm_kernel, u, w['proj_ssm'], (BF16, BF16, F32), tm, "proj_ssm")
    qs, ks, vs, gs, ga = _proj_attn_wide(u, w['proj_attn'], s, tm)
    r3 = lambda a: a.reshape(b, s, a.shape[-1])
    xbc3 = r3(xbc)
    y_ssm, ssm_new = _ssd_prompt(r3(z), xbc3, r3(dt), w['conv_w'], w['conv_b'], w['dt_bias'], w['a_log'],
                                 w['d_skip'], w['norm_ssm'])
    order = sorted(range(len(DIL_GROUPS)), key=lambda gi: -DIL_GROUPS[gi][1])
    assert all(DIL_GROUPS[a][1] == MERGE_FAN * DIL_GROUPS[c][1] for a, c in zip(order, order[1:]))
    assert DIL_GROUPS[order[-1]][1] == 1
    o, lse = None, None
    for j, gi in enumerate(order):
        o, lse = _attn_prompt_group(qs[gi], ks[gi], vs[gi], b, gi, j == 0, j == len(order) - 1, o, lse)
    y = _out_stage(h1, y_ssm.reshape(t, -1), o, gs, ga, p_emb.reshape(t, -1),
                   w['out'] + (_row(norm_final),), OUT_ROW_TILE)
    kv = [_kv_stack_wide(ks[gi], vs[gi], b, dil, min(window, s)) for gi, (window, dil) in enumerate(DIL_GROUPS)]
    conv_new = xbc3[:, s - (CONV_W - 1):, :].astype(F32)
    return y.reshape(b, s, d), kv, conv_new, ssm_new


def _sample_layer(x, p_emb, w, norm_final, conv_prev, ssm_prev, caches):
    nb, s, d = x.shape
    assert s == 1
    h1, u = _ffn(x.reshape(nb, d), *w['ffn1'], nb)
    z, xbc, dt = _proj(_proj_ssm_kernel, u, w['proj_ssm'], (F32, F32, F32), nb, "proj_ssm_sample")
    q, k, v, gs, ga = _proj(functools.partial(_proj_attn_kernel, seq=1, pos0=PAST_LEN), u, w['proj_attn'],
                            (F32,) * 5, nb, "proj_attn_sample")
    xs, xdt_t, bdt, c_t, dec = _ssd_sample_prep(xbc, conv_prev.reshape(nb, -1), dt, w['conv_w'], w['conv_b'],
                                                w['dt_bias'], w['a_log'], w['d_inner'])
    dec_flat = dec[:, :ssm_prev.shape[1]].reshape(-1)
    ssm_new, y_t = _ssd_sample_state(dec_flat, ssm_prev, xdt_t, bdt, c_t)
    y_ssm = _ssd_sample_post(y_t, xs, z, w['d_skip'], w['norm_ssm'])
    by_head = lambda a: a.reshape(nb, -1, HEAD_DIM)
    attn = _attn_sample(by_head(q), by_head(k), by_head(v), caches)
    y = _out_stage(h1, y_ssm, attn.reshape(nb, -1), gs, ga, p_emb.reshape(nb, -1),
                   w['out'] + (_row(norm_final),), nb)
    r3 = lambda a: a.reshape(nb, 1, a.shape[-1])
    kv = [_kv_stack(r3(k), r3(v), gi, 1) for gi in range(len(DIL_GROUPS))]
    conv_new = jnp.concatenate([conv_prev[:, 1:], xbc[:, None, :]], axis=1)
    return y.reshape(nb, 1, d), kv, conv_new, ssm_new


PROMPT_ROW_TILE = 512
OUT_ROW_TILE = 512


def kernel(x_prompt, x_sample, cache_kv_w128, cache_kv_w512, cache_kv_w2048, state_conv, state_ssm, p_prompt, p_sample, norm_ffn1, w_ffn1_gate, w_ffn1_up, w_ffn1_down, norm_mix, w_in, conv_w, conv_b, dt_bias, a_log, d_skip, norm_ssm, w_o_ssm, w_o_attn, w_out, norm_ffn2, w_ffn2_gate, w_ffn2_up, w_ffn2_down, norm_ple, w_ple_gate, w_ple_proj, norm_final):
    depth = w_in.shape[0]
    assert depth == 1
    layer_params = dict(
        norm_ffn1=norm_ffn1, w_ffn1_gate=w_ffn1_gate, w_ffn1_up=w_ffn1_up, w_ffn1_down=w_ffn1_down,
        norm_mix=norm_mix, w_in=w_in, conv_w=conv_w, conv_b=conv_b, dt_bias=dt_bias, a_log=a_log,
        d_skip=d_skip, norm_ssm=norm_ssm, w_o_ssm=w_o_ssm, w_o_attn=w_o_attn, w_out=w_out,
        norm_ffn2=norm_ffn2, w_ffn2_gate=w_ffn2_gate, w_ffn2_up=w_ffn2_up, w_ffn2_down=w_ffn2_down,
        norm_ple=norm_ple, w_ple_gate=w_ple_gate, w_ple_proj=w_ple_proj)
    i = 0
    w = _layer_weights({name: val[i] for name, val in layer_params.items()})
    yp, kvp, convp, ssmp = _prompt_layer(x_prompt, p_prompt[i], w, norm_final, PROMPT_ROW_TILE)
    ys, kvs, convs, ssms = _sample_layer(x_sample, p_sample[i], w, norm_final, state_conv[i], state_ssm[i],
                                         (cache_kv_w128[i], cache_kv_w512[i], cache_kv_w2048[i]))
    st = lambda a: a[None]
    return (yp, ys, st(kvp[0]), st(kvp[1]), st(kvp[2]), st(convp), st(ssmp),
            st(kvs[0]), st(kvs[1]), st(kvs[2]), st(convs), st(ssms))
```
